```python
import jax, jax.numpy as jnp
from jax import lax
import numpy as np

D_MODEL = 1024
BATCH = 8
SEQ = 4096
DEPTH = 2
DEC_BATCH = 32
DEC_SEQ = 1
PAST_LEN = 16384
PAGE_SIZE = 128

N_META = 16
HEAD_DIM = 64
H_A = 8
H_B = 8
H_C = 16
D_A = H_A * HEAD_DIM
D_B = H_B * HEAD_DIM
D_C = H_C * HEAD_DIM
LORA_W = 64
LORA_A = 64
LORA_G = 128
D_SHIFT = 3 * D_A + LORA_W + LORA_A + LORA_G
D_IN_EVEN = D_SHIFT + 3 * D_B
D_IN_ODD = 3 * D_C + H_C
D_FF = 2816
BLOCK = 128
N_EVEN = (DEPTH + 1) // 2
N_ODD = DEPTH // 2
FFN_RES = 0.5
NORM_EPS = 1e-6
GN_EPS = 64e-5
NEG_INF = -1e30
LEAD = (-N_META) % BLOCK

kernel_name = 'rwkv7_stickbreak_fox_hybrid_step'


def rms_norm(x, g):
    xf = x.astype(jnp.float32)
    y = xf * lax.rsqrt(jnp.mean(xf * xf, axis=-1, keepdims=True) + NORM_EPS)
    return (y * g.astype(jnp.float32)).astype(x.dtype)


def swiglu(x, w1, w3, w2):
    return (jax.nn.silu(x @ w1) * (x @ w3)) @ w2


def half_ffn(h, g_pre, g_post, w1, w3, w2):
    return FFN_RES * rms_norm(swiglu(rms_norm(h, g_pre), w1, w3, w2), g_post)


def heads(t, n_heads):
    return t.reshape(*t.shape[:-1], n_heads, HEAD_DIM)


def pad_front(t):
    return jnp.pad(t, [(0, 0), (LEAD, 0)] + [(0, 0)] * (t.ndim - 2))


def gather_pages(pool, layer, page_table):
    g = pool[layer, page_table]
    return g.reshape(g.shape[0], g.shape[1] * g.shape[2], *g.shape[3:])


def sweep_query_blocks(fn, qpos, *per_query):
    nb = qpos.shape[0] // BLOCK
    def blocks(a):
        return jnp.moveaxis(a.reshape(a.shape[0], nb, BLOCK, *a.shape[2:]), 1, 0)
    out = lax.map(lambda args: fn(*args), (qpos.reshape(nb, BLOCK),) + tuple(blocks(a) for a in per_query))
    out = jnp.moveaxis(out, 0, 1)
    return out.reshape(out.shape[0], nb * BLOCK, *out.shape[3:])


def rwkv7_recurrence(S0, r, w, k, v, a, b):
    def step(S, inp):
        r_t, w_t, k_t, v_t, a_t, b_t = inp
        sa = jnp.einsum('bhvk,bhk->bhv', S, a_t)
        S = S * w_t[:, :, None, :] + sa[..., None] * b_t[:, :, None, :] + v_t[..., None] * k_t[:, :, None, :]
        return S, jnp.einsum('bhvk,bhk->bhv', S, r_t)
    xs = tuple(jnp.moveaxis(t.astype(jnp.float32), 1, 0) for t in (r, w, k, v, a, b))
    S, ys = lax.scan(step, S0.astype(jnp.float32), xs)
    return S, jnp.moveaxis(ys, 0, 1)


def rwkv7_mix(p, prev, S0, mu, w0, w2, a0, a2, g2, k_k, k_a, r_k, ln_w, ln_b):
    B, T = p.shape[0], p.shape[1]
    f32 = jnp.float32
    p_prev = jnp.concatenate([prev[:, None].astype(p.dtype), p[:, :-1]], axis=1)
    ps = p + (p_prev - p) * mu
    r, k, v, lw, la, lg = jnp.split(ps, [D_A, 2 * D_A, 3 * D_A, 3 * D_A + LORA_W, 3 * D_A + LORA_W + LORA_A], axis=-1)
    w_log = -jax.nn.softplus(-(w0 + jnp.tanh(lw) @ w2).astype(f32)) - 0.5
    decay = jnp.exp(-jnp.exp(w_log))
    a = jax.nn.sigmoid((a0 + la @ a2).astype(f32))
    g = (jax.nn.sigmoid(lg) @ g2).astype(f32)
    kk = heads((k * k_k).astype(f32), H_A)
    kk = kk / jnp.maximum(jnp.linalg.norm(kk, axis=-1, keepdims=True), 1e-12)
    k = k.astype(f32) * (1.0 + (a - 1.0) * k_a.astype(f32))
    rh, kh, vh, ah = heads(r.astype(f32), H_A), heads(k, H_A), heads(v.astype(f32), H_A), heads(a, H_A)
    S, y = rwkv7_recurrence(S0, rh, heads(decay, H_A), kh, vh, -kk, kk * ah)
    mean = jnp.mean(y, axis=-1, keepdims=True)
    var = jnp.mean(jnp.square(y - mean), axis=-1, keepdims=True)
    y = ((y - mean) * lax.rsqrt(var + GN_EPS)).reshape(B, T, D_A) * ln_w.astype(f32) + ln_b.astype(f32)
    y = y + (jnp.sum(rh * kh * r_k.astype(f32), axis=-1, keepdims=True) * vh).reshape(B, T, D_A)
    return (y * g).astype(p.dtype), S.astype(S0.dtype), p[:, -1]


def stick_breaking_attention(q, qpos, segs):
    qf = q.astype(jnp.float32) * HEAD_DIM ** -0.5
    z = jnp.concatenate([jnp.einsum('bqhd,bkhd->bhqk', qf, k.astype(jnp.float32)) for k, _, _, _ in segs], axis=-1)
    vis = jnp.concatenate([(kp[None, :] < qpos[:, None]) & kv[None, :] for _, _, kp, kv in segs], axis=-1)
    log_rest = jnp.where(vis, jax.nn.log_sigmoid(-z), 0.0)
    csum = jnp.cumsum(log_rest, axis=-1)
    A = jnp.where(vis, jnp.exp(jax.nn.log_sigmoid(z) + csum[..., -1:] - csum), 0.0)
    out = 0.0
    start = 0
    for k, v, _, _ in segs:
        n = k.shape[1]
        out = out + jnp.einsum('bhqk,bkhd->bqhd', A[..., start:start + n], v.astype(jnp.float32))
        start += n
    return out


def forgetting_attention(q, qpos, cq, segs):
    qf = q.astype(jnp.float32) * HEAD_DIM ** -0.5
    cq_t = jnp.swapaxes(cq.astype(jnp.float32), 1, 2)[..., :, None]
    z = jnp.concatenate([jnp.einsum('bqhd,bkhd->bhqk', qf, k.astype(jnp.float32)) + cq_t
                         - jnp.swapaxes(ck.astype(jnp.float32), 1, 2)[..., None, :] for k, _, ck, _, _ in segs], axis=-1)
    vis = jnp.concatenate([(kp[None, :] <= qpos[:, None]) & kv[None, :] for _, _, _, kp, kv in segs], axis=-1)
    P = jax.nn.softmax(jnp.where(vis, z, NEG_INF), axis=-1)
    out = 0.0
    start = 0
    for k, v, _, _, _ in segs:
        n = k.shape[1]
        out = out + jnp.einsum('bhqk,bkhd->bqhd', P[..., start:start + n], v.astype(jnp.float32))
        start += n
    return out


def setup_inputs(seed: int = 0) -> dict:
    key = jax.random.key(seed)
    ks = iter(jax.random.split(key, 48))
    def nrm(shape, scale=1.0):
        return jax.random.normal(next(ks), shape, jnp.float32) * scale
    def unif(shape, lo, hi):
        return jax.random.uniform(next(ks), shape, jnp.float32, lo, hi)
    n_pages = PAST_LEN // PAGE_SIZE
    n_used = DEC_BATCH * n_pages
    n_pool = n_used + n_used // 4
    page_table = jax.random.permutation(next(ks), n_pool)[:n_used].reshape(DEC_BATCH, n_pages).astype(jnp.int32)
    return {
        'x_prompt': nrm((BATCH, SEQ, D_MODEL)),
        'x_sample': nrm((DEC_BATCH, DEC_SEQ, D_MODEL)),
        'cache_sb_k': nrm((N_EVEN, n_pool, PAGE_SIZE, H_B, HEAD_DIM)),
        'cache_sb_v': nrm((N_EVEN, n_pool, PAGE_SIZE, H_B, HEAD_DIM)),
        'cache_fox_k': nrm((N_ODD, n_pool, PAGE_SIZE, H_C, HEAD_DIM)),
        'cache_fox_v': nrm((N_ODD, n_pool, PAGE_SIZE, H_C, HEAD_DIM)),
        'cache_fox_logf': jax.nn.log_sigmoid(2.0 + nrm((N_ODD, n_pool, PAGE_SIZE, H_C))),
        'state_rwkv': nrm((N_EVEN, DEC_BATCH, H_A, HEAD_DIM, HEAD_DIM), 0.3),
        'state_rwkv_shift': nrm((N_EVEN, DEC_BATCH, D_SHIFT)),
        'page_table': page_table,
        'meta_tokens': nrm((N_META, D_MODEL)),
        'norm_g': 1.0 + nrm((DEPTH, 6, D_MODEL), 0.05),
        'ffn_w1': nrm((DEPTH, 2, D_MODEL, D_FF), D_MODEL ** -0.5),
        'ffn_w3': nrm((DEPTH, 2, D_MODEL, D_FF), D_MODEL ** -0.5),
        'ffn_w2': nrm((DEPTH, 2, D_FF, D_MODEL), D_FF ** -0.5),
        'w_in_even': nrm((N_EVEN, D_MODEL, D_IN_EVEN), D_MODEL ** -0.5),
        'w_out_even': nrm((N_EVEN, D_A + D_B, D_MODEL), (D_A + D_B) ** -0.5),
        'rwkv_mu': unif((N_EVEN, D_SHIFT), 0.0, 1.0),
        'rwkv_w0': unif((N_EVEN, D_A), -6.0, -1.0),
        'rwkv_w2': nrm((N_EVEN, LORA_W, D_A), 0.1 * LORA_W ** -0.5),
        'rwkv_a0': nrm((N_EVEN, D_A), 0.5),
        'rwkv_a2': nrm((N_EVEN, LORA_A, D_A), LORA_A ** -0.5),
        'rwkv_g2': nrm((N_EVEN, LORA_G, D_A), LORA_G ** -0.5),
        'rwkv_kk': 0.85 + nrm((N_EVEN, D_A), 0.05),
        'rwkv_ka': 1.0 + nrm((N_EVEN, D_A), 0.05),
        'rwkv_rk': nrm((N_EVEN, H_A, HEAD_DIM), 0.1),
        'rwkv_lnw': 1.0 + nrm((N_EVEN, D_A), 0.05),
        'rwkv_lnb': nrm((N_EVEN, D_A), 0.02),
        'w_in_odd': nrm((N_ODD, D_MODEL, D_IN_ODD), D_MODEL ** -0.5),
        'b_f': 2.0 + nrm((N_ODD, H_C), 0.5),
        'w_out_odd': nrm((N_ODD, D_C, D_MODEL), D_C ** -0.5),
    }


def reference(x_prompt, x_sample, cache_sb_k, cache_sb_v, cache_fox_k, cache_fox_v, cache_fox_logf,
              state_rwkv, state_rwkv_shift, page_table, meta_tokens, norm_g, ffn_w1, ffn_w3, ffn_w2,
              w_in_even, w_out_even, rwkv_mu, rwkv_w0, rwkv_w2, rwkv_a0, rwkv_a2, rwkv_g2, rwkv_kk, rwkv_ka,
              rwkv_rk, rwkv_lnw, rwkv_lnb, w_in_odd, b_f, w_out_odd):
    B = x_prompt.shape[0]
    DB, T_S = x_sample.shape[0], x_sample.shape[1]
    dt_p, dt_s = x_prompt.dtype, x_sample.dtype
    h_p = jnp.concatenate([jnp.broadcast_to(meta_tokens[None].astype(dt_p), (B, N_META, D_MODEL)), x_prompt], axis=1)
    h_s = x_sample
    L_pad = LEAD + h_p.shape[1]
    pos_p = jnp.arange(L_pad) - LEAD
    valid_p = pos_p >= 0
    past_len = page_table.shape[1] * PAGE_SIZE
    pos_past = jnp.arange(past_len)
    valid_past = jnp.ones((past_len,), bool)
    pos_s = past_len + jnp.arange(T_S)
    valid_s = jnp.ones((T_S,), bool)

    p_sb_k, p_sb_v, p_fk, p_fv, p_fl, p_rw, p_sh = [], [], [], [], [], [], []
    s_sb_k, s_sb_v, s_fk, s_fv, s_fl, s_rw, s_sh = [], [], [], [], [], [], []

    for l in range(DEPTH):
        ng = norm_g[l]
        h_p = h_p + half_ffn(h_p, ng[0], ng[1], ffn_w1[l, 0], ffn_w3[l, 0], ffn_w2[l, 0])
        h_s = h_s + half_ffn(h_s, ng[0], ng[1], ffn_w1[l, 0], ffn_w3[l, 0], ffn_w2[l, 0])
        u_p = rms_norm(h_p, ng[2])
        u_s = rms_norm(h_s, ng[2])
        if l % 2 == 0:
            e = l // 2
            rw = (rwkv_mu[e], rwkv_w0[e], rwkv_w2[e], rwkv_a0[e], rwkv_a2[e], rwkv_g2[e],
                  rwkv_kk[e], rwkv_ka[e], rwkv_rk[e], rwkv_lnw[e], rwkv_lnb[e])
            proj = u_p @ w_in_even[e]
            ya, S_new, sh_new = rwkv7_mix(proj[..., :D_SHIFT], jnp.zeros((B, D_SHIFT), dt_p),
                                          jnp.zeros((B, H_A, HEAD_DIM, HEAD_DIM), state_rwkv.dtype), *rw)
            q, k, v = [heads(t, H_B) for t in jnp.split(proj[..., D_SHIFT:], 3, axis=-1)]
            segs = [(pad_front(k), pad_front(v), pos_p, valid_p)]
            yb = sweep_query_blocks(lambda qp, qb: stick_breaking_attention(qb, qp, segs), pos_p, pad_front(q))[:, LEAD:]
            m_p = jnp.concatenate([ya, yb.reshape(B, -1, D_B).astype(dt_p)], axis=-1) @ w_out_even[e]
            p_sb_k.append(k); p_sb_v.append(v); p_rw.append(S_new); p_sh.append(sh_new)
            proj = u_s @ w_in_even[e]
            ya, S_new, sh_new = rwkv7_mix(proj[..., :D_SHIFT], state_rwkv_shift[e], state_rwkv[e], *rw)
            q, k, v = [heads(t, H_B) for t in jnp.split(proj[..., D_SHIFT:], 3, axis=-1)]
            segs = [(gather_pages(cache_sb_k, e, page_table), gather_pages(cache_sb_v, e, page_table), pos_past, valid_past),
                    (k, v, pos_s, valid_s)]
            yb = stick_breaking_attention(q, pos_s, segs)
            m_s = jnp.concatenate([ya, yb.reshape(DB, T_S, D_B).astype(dt_s)], axis=-1) @ w_out_even[e]
            s_sb_k.append(k); s_sb_v.append(v); s_rw.append(S_new); s_sh.append(sh_new)
        else:
            o = l // 2
            q, k, v, fl = jnp.split(u_p @ w_in_odd[o], [D_C, 2 * D_C, 3 * D_C], axis=-1)
            q, k, v = heads(q, H_C), heads(k, H_C), heads(v, H_C)
            logf = jax.nn.log_sigmoid((fl + b_f[o]).astype(jnp.float32))
            cum = jnp.cumsum(pad_front(logf), axis=1)
            segs = [(pad_front(k), pad_front(v), cum, pos_p, valid_p)]
            yc = sweep_query_blocks(lambda qp, qb, cb: forgetting_attention(qb, qp, cb, segs),
                                    pos_p, pad_front(q), cum)[:, LEAD:]
            m_p = yc.reshape(B, -1, D_C).astype(dt_p) @ w_out_odd[o]
            p_fk.append(k); p_fv.append(v); p_fl.append(logf.astype(cache_fox_logf.dtype))
            q, k, v, fl = jnp.split(u_s @ w_in_odd[o], [D_C, 2 * D_C, 3 * D_C], axis=-1)
            q, k, v = heads(q, H_C), heads(k, H_C), heads(v, H_C)
            logf = jax.nn.log_sigmoid((fl + b_f[o]).astype(jnp.float32))
            cum_past = jnp.cumsum(gather_pages(cache_fox_logf, o, page_table).astype(jnp.float32), axis=1)
            cum_new = cum_past[:, -1:] + jnp.cumsum(logf, axis=1)
            segs = [(gather_pages(cache_fox_k, o, page_table), gather_pages(cache_fox_v, o, page_table), cum_past, pos_past, valid_past),
                    (k, v, cum_new, pos_s, valid_s)]
            yc = forgetting_attention(q, pos_s, cum_new, segs)
            m_s = yc.reshape(DB, T_S, D_C).astype(dt_s) @ w_out_odd[o]
            s_fk.append(k); s_fv.append(v); s_fl.append(logf.astype(cache_fox_logf.dtype))
        h_p = h_p + rms_norm(m_p, ng[3])
        h_s = h_s + rms_norm(m_s, ng[3])
        h_p = h_p + half_ffn(h_p, ng[4], ng[5], ffn_w1[l, 1], ffn_w3[l, 1], ffn_w2[l, 1])
        h_s = h_s + half_ffn(h_s, ng[4], ng[5], ffn_w1[l, 1], ffn_w3[l, 1], ffn_w2[l, 1])

    y_prompt = h_p[:, N_META:]
    y_sample = h_s
    return (y_prompt, y_sample,
            jnp.stack(p_sb_k), jnp.stack(p_sb_v), jnp.stack(p_fk), jnp.stack(p_fv), jnp.stack(p_fl),
            jnp.stack(p_rw), jnp.stack(p_sh),
            jnp.stack(s_sb_k), jnp.stack(s_sb_v), jnp.stack(s_fk), jnp.stack(s_fv), jnp.stack(s_fl),
            jnp.stack(s_rw), jnp.stack(s_sh))
```

```python
import functools

import jax
import jax.numpy as jnp
from jax import lax
from jax.experimental import pallas as pl
from jax.experimental.pallas import tpu as pltpu

F32 = jnp.float32
BF16 = jnp.bfloat16

HEAD_DIM = 64
N_META = 16
ATTN_BLOCK = 128
LEAD = (-N_META) % ATTN_BLOCK
PAGE_SIZE = 128
LORA_W = 64
LORA_A = 64
LORA_G = 128
FFN_RES = 0.5
NORM_EPS = 1e-6
GN_EPS = 64e-5
NEG_INF = -1e30
LANES = 128
RWKV_CHUNK = 64
VMEM_LIMIT = 52 * 1024 * 1024


def _params(sem):
    return pltpu.CompilerParams(dimension_semantics=sem, vmem_limit_bytes=VMEM_LIMIT)


def _row_tile(m, candidates=(384, 256, 128, 64, 32, 16, 8)):
    for t in candidates:
        if m % t == 0:
            return t
    raise ValueError(f"no row tile for {m}")


def _rms(x, g):
    return x * lax.rsqrt(jnp.mean(x * x, axis=-1, keepdims=True) + NORM_EPS) * g


def _split3(x):
    h = x.astype(BF16)
    r = x - h.astype(F32)
    m = r.astype(BF16)
    l = (r - m.astype(F32)).astype(BF16)
    return h, m, l


def _split2(x):
    h = x.astype(BF16)
    return h, (x - h.astype(F32)).astype(BF16)


def _dot(a, b):
    return jnp.dot(a, b, preferred_element_type=F32)


def _dot_nt(a, b):
    return lax.dot_general(a, b, (((1,), (1,)), ((), ())), preferred_element_type=F32)


def _dot_tn(a, b):
    return lax.dot_general(a, b, (((0,), (0,)), ((), ())), preferred_element_type=F32)


def _ffn_kernel(h_ref, gpre_ref, gpost_ref, w1_ref, w3_ref, w2_ref, o_ref, xn_ref, acc_ref):
    j = pl.program_id(1)

    @pl.when(j == 0)
    def _():
        xn_ref[...] = _rms(h_ref[...], gpre_ref[...]).astype(BF16)
        acc_ref[...] = jnp.zeros_like(acc_ref)

    x = xn_ref[...]
    a = _dot(x, w1_ref[...])
    b = _dot(x, w3_ref[...])
    t = (a * jax.nn.sigmoid(a) * b).astype(BF16)
    acc_ref[...] += _dot(t, w2_ref[...])

    @pl.when(j == pl.num_programs(1) - 1)
    def _():
        o_ref[...] = h_ref[...] + FFN_RES * _rms(acc_ref[...], gpost_ref[...])


def ffn_half(h, g_pre, g_post, w1, w3, w2, *, tm, tf):
    m, d = h.shape
    dff = w1.shape[1]
    return pl.pallas_call(
        _ffn_kernel,
        grid=(m // tm, dff // tf),
        in_specs=[
            pl.BlockSpec((tm, d), lambda i, j: (i, 0)),
            pl.BlockSpec((1, d), lambda i, j: (0, 0)),
            pl.BlockSpec((1, d), lambda i, j: (0, 0)),
            pl.BlockSpec((d, tf), lambda i, j: (0, j)),
            pl.BlockSpec((d, tf), lambda i, j: (0, j)),
            pl.BlockSpec((tf, d), lambda i, j: (j, 0)),
        ],
        out_specs=pl.BlockSpec((tm, d), lambda i, j: (i, 0)),
        out_shape=jax.ShapeDtypeStruct((m, d), F32),
        scratch_shapes=[pltpu.VMEM((tm, d), BF16), pltpu.VMEM((tm, d), F32)],
        compiler_params=_params(("parallel", "arbitrary")),
        name="ffn_half",
    )(h, g_pre, g_post, w1, w3, w2)


def _proj_kernel(*refs, out_dtypes, scales):
    h_ref, g_ref = refs[0], refs[1]
    n_w = len(out_dtypes)
    w_refs = refs[2:2 + n_w]
    o_refs = iter(refs[2 + n_w:])
    x = _rms(h_ref[...], g_ref[...]).astype(BF16)
    for w_ref, dts, s in zip(w_refs, out_dtypes, scales):
        y = _dot(x, w_ref[...])
        if s != 1.0:
            y = y * s
        for dt in dts:
            next(o_refs)[...] = y.astype(dt)


def norm_proj(h, g, weights, out_dtypes, scales, *, tm):
    m, d = h.shape
    in_specs = [pl.BlockSpec((tm, d), lambda i: (i, 0)), pl.BlockSpec((1, d), lambda i: (0, 0))]
    in_specs += [pl.BlockSpec(w.shape, lambda i: (0, 0)) for w in weights]
    outs = [(w.shape[1], dt) for w, dts in zip(weights, out_dtypes) for dt in dts]
    return pl.pallas_call(
        functools.partial(_proj_kernel, out_dtypes=tuple(tuple(d) for d in out_dtypes), scales=tuple(scales)),
        grid=(m // tm,),
        in_specs=in_specs,
        out_specs=[pl.BlockSpec((tm, n), lambda i: (i, 0)) for n, _ in outs],
        out_shape=[jax.ShapeDtypeStruct((m, n), dt) for n, dt in outs],
        compiler_params=_params(("parallel",)),
        name="norm_proj",
    )(h, g, *weights)


def _outproj_kernel(*refs, n_in, tm, lead):
    h_ref, g_ref = refs[0], refs[1]
    y_refs = refs[2:2 + n_in]
    w_refs = refs[2 + n_in:2 + 2 * n_in]
    o_ref = refs[2 + 2 * n_in]
    m = _dot(y_refs[0][0], w_refs[0][...])
    for y_ref, w_ref in zip(y_refs[1:], w_refs[1:]):
        m = m + _dot(y_ref[0], w_ref[...])
    out = h_ref[0] + _rms(m, g_ref[...])
    pos = pl.program_id(1) * tm + lax.broadcasted_iota(jnp.int32, (tm, 1), 0)
    o_ref[0] = jnp.where(pos >= lead, out, 0.0)


def out_proj(h, g, ys, ws, *, tm, lead):
    b, lp, d = h.shape
    n = len(ys)
    in_specs = [pl.BlockSpec((1, tm, d), lambda i, j: (i, j, 0)), pl.BlockSpec((1, d), lambda i, j: (0, 0))]
    in_specs += [pl.BlockSpec((1, tm, y.shape[2]), lambda i, j: (i, j, 0)) for y in ys]
    in_specs += [pl.BlockSpec(w.shape, lambda i, j: (0, 0)) for w in ws]
    return pl.pallas_call(
        functools.partial(_outproj_kernel, n_in=n, tm=tm, lead=lead),
        grid=(b, lp // tm),
        in_specs=in_specs,
        out_specs=pl.BlockSpec((1, tm, d), lambda i, j: (i, j, 0)),
        out_shape=jax.ShapeDtypeStruct((b, lp, d), F32),
        compiler_params=_params(("parallel", "parallel")),
        name="out_proj",
    )(h, g, *ys, *ws)


def _sb_attn_kernel(q_ref, k_ref, v_ref, u2_ref, o_ref, *, tq, tk, lead):
    q0 = pl.program_id(2) * tq
    nkb = (q0 + tq) // tk
    lane = lax.broadcasted_iota(jnp.int32, (1, LANES), 1)
    qrow = q0 + lax.broadcasted_iota(jnp.int32, (tq, tk), 0)
    kcol = lax.broadcasted_iota(jnp.int32, (tq, tk), 1)
    q = q_ref[0]
    u2 = u2_ref[...]
    out = jnp.zeros((tq, LANES), F32)
    for hh in range(2):
        head_lanes = (lane >= hh * HEAD_DIM) & (lane < (hh + 1) * HEAD_DIM)
        qm = jnp.where(head_lanes, q, jnp.zeros_like(q))

        def body(jj, carry, qm=qm):
            acc, run = carry
            ks = pl.multiple_of((nkb - 1 - jj) * tk, tk)
            kblk = k_ref[0, pl.ds(ks, tk), :]
            vblk = v_ref[0, pl.ds(ks, tk), :]
            z = _dot_nt(qm, kblk)
            kpos = ks + kcol
            vis = (kpos < qrow) & (kpos >= lead)
            ls = jnp.minimum(z, 0.0) - jnp.log1p(jnp.exp(-jnp.abs(z)))
            lr = jnp.where(vis, ls - z, 0.0)
            hi, lo = _split2(lr)
            s2 = _dot(jnp.concatenate([hi, lo], axis=1), u2)
            a = jnp.where(vis, jnp.exp(ls + s2[:, :tk] + run), 0.0)
            acc = acc + _dot(a.astype(BF16), vblk)
            return acc, run + s2[:, tk:]

        zero = jnp.zeros((tq, LANES), F32)
        acc, _ = lax.fori_loop(0, nkb, body, (zero, zero))
        out = jnp.where(head_lanes, acc, out)
    o_ref[0] = out.astype(o_ref.dtype)


def _suffix_matrix(tk):
    j = jnp.arange(tk)[:, None]
    s = jnp.arange(tk)[None, :]
    u = jnp.concatenate([(j > s).astype(BF16), jnp.ones((tk, tk), BF16)], axis=1)
    return jnp.concatenate([u, u], axis=0)


def sb_attention(q, k, v, *, tq, lead):
    b, lp, dq = q.shape
    tk = LANES
    u2 = _suffix_matrix(tk)
    return pl.pallas_call(
        functools.partial(_sb_attn_kernel, tq=tq, tk=tk, lead=lead),
        grid=(b, dq // LANES, lp // tq),
        in_specs=[
            pl.BlockSpec((1, tq, LANES), lambda i, p, j: (i, j, p)),
            pl.BlockSpec((1, lp, LANES), lambda i, p, j: (i, 0, p)),
            pl.BlockSpec((1, lp, LANES), lambda i, p, j: (i, 0, p)),
            pl.BlockSpec(u2.shape, lambda i, p, j: (0, 0)),
        ],
        out_specs=pl.BlockSpec((1, tq, LANES), lambda i, p, j: (i, j, p)),
        out_shape=jax.ShapeDtypeStruct((b, lp, dq), BF16),
        compiler_params=_params(("parallel", "parallel", "arbitrary")),
        name="sb_attention",
    )(q, k, v, u2)


def _fox_attn_kernel(q_ref, k_ref, v_ref, cq_ref, ckt_ref, o_ref, *, tq, tk, lead):
    pair = pl.program_id(1)
    q0 = pl.program_id(2) * tq
    nkb = (q0 + tq) // tk
    lane = lax.broadcasted_iota(jnp.int32, (1, LANES), 1)
    qrow = q0 + lax.broadcasted_iota(jnp.int32, (tq, tk), 0)
    kcol = lax.broadcasted_iota(jnp.int32, (tq, tk), 1)
    q = q_ref[0]
    cq_all = cq_ref[0]
    out = jnp.zeros((tq, LANES), F32)
    for hh in range(2):
        h = pair * 2 + hh
        head_lanes = (lane >= hh * HEAD_DIM) & (lane < (hh + 1) * HEAD_DIM)
        qm = jnp.where(head_lanes, q, jnp.zeros_like(q))
        cq = jnp.sum(jnp.where(lane == h, cq_all, 0.0), axis=1, keepdims=True)
        cqb = jnp.broadcast_to(cq, (tq, tk))

        def body(jb, carry, qm=qm, cqb=cqb, h=h):
            acc, m, l = carry
            ks = pl.multiple_of(jb * tk, tk)
            kblk = k_ref[0, pl.ds(ks, tk), :]
            vblk = v_ref[0, pl.ds(ks, tk), :]
            ck = ckt_ref[0, hh, :, pl.ds(ks, tk)]
            z = _dot_nt(qm, kblk) + cqb - ck
            kpos = ks + kcol
            vis = (kpos <= qrow) & (kpos >= lead)
            z = jnp.where(vis, z, NEG_INF)
            m_new = jnp.maximum(m, jnp.max(z, axis=1, keepdims=True))
            alpha = jnp.exp(m - m_new)
            p = jnp.exp(z - m_new)
            l = alpha * l + jnp.sum(p, axis=1, keepdims=True)
            acc = alpha * acc + _dot(p.astype(BF16), vblk)
            return acc, m_new, l

        init = (jnp.zeros((tq, LANES), F32), jnp.full((tq, 1), NEG_INF, F32), jnp.zeros((tq, 1), F32))
        acc, _, l = lax.fori_loop(0, nkb, body, init)
        out = jnp.where(head_lanes, acc / l, out)
    o_ref[0] = out.astype(o_ref.dtype)


def fox_attention(q, k, v, cum, cum_t, *, tq, lead):
    b, lp, dq = q.shape
    tk = LANES
    return pl.pallas_call(
        functools.partial(_fox_attn_kernel, tq=tq, tk=tk, lead=lead),
        grid=(b, dq // LANES, lp // tq),
        in_specs=[
            pl.BlockSpec((1, tq, LANES), lambda i, p, j: (i, j, p)),
            pl.BlockSpec((1, lp, LANES), lambda i, p, j: (i, 0, p)),
            pl.BlockSpec((1, lp, LANES), lambda i, p, j: (i, 0, p)),
            pl.BlockSpec((1, tq, LANES), lambda i, p, j: (i, j, 0)),
            pl.BlockSpec((1, 2, 1, lp), lambda i, p, j: (i, p, 0, 0)),
        ],
        out_specs=pl.BlockSpec((1, tq, LANES), lambda i, p, j: (i, j, p)),
        out_shape=jax.ShapeDtypeStruct((b, lp, dq), BF16),
        compiler_params=_params(("parallel", "parallel", "arbitrary")),
        name="fox_attention",
    )(q, k, v, cum, cum_t)


def _logf_cum_kernel(fl_ref, bf_ref, logf_ref, cum_ref, cumt_ref, carry_ref, carryt_ref, *, lead):
    j = pl.program_id(1)

    @pl.when(j == 0)
    def _():
        carry_ref[...] = jnp.zeros_like(carry_ref)
        carryt_ref[...] = jnp.zeros_like(carryt_ref)

    x = fl_ref[0] + bf_ref[...]
    logf = jnp.minimum(x, 0.0) - jnp.log1p(jnp.exp(-jnp.abs(x)))
    logf_ref[0] = logf
    tb = logf.shape[0]
    pos = j * tb + lax.broadcasted_iota(jnp.int32, (tb, 1), 0)
    lz = jnp.where(pos >= lead, logf, 0.0)
    ri = lax.broadcasted_iota(jnp.int32, (tb, tb), 0)
    ci = lax.broadcasted_iota(jnp.int32, (tb, tb), 1)
    tril = (ci <= ri).astype(BF16)
    triu = (ri <= ci).astype(BF16)
    c = carry_ref[...]
    ct = carryt_ref[...]
    for p in _split3(lz):
        c = c + _dot(tril, p)
        ct = ct + _dot_tn(p, triu)
    cum_ref[0] = c
    cumt_ref[0] = ct
    carry_ref[...] = jnp.broadcast_to(c[tb - 1:tb, :], carry_ref.shape)
    carryt_ref[...] = jnp.broadcast_to(ct[:, tb - 1:tb], carryt_ref.shape)


def logf_cumsum(fl, b_f, *, lead):
    b, lp, _ = fl.shape
    tb = LANES
    return pl.pallas_call(
        functools.partial(_logf_cum_kernel, lead=lead),
        grid=(b, lp // tb),
        in_specs=[
            pl.BlockSpec((1, tb, LANES), lambda i, j: (i, j, 0)),
            pl.BlockSpec((1, LANES), lambda i, j: (0, 0)),
        ],
        out_specs=[
            pl.BlockSpec((1, tb, LANES), lambda i, j: (i, j, 0)),
            pl.BlockSpec((1, tb, LANES), lambda i, j: (i, j, 0)),
            pl.BlockSpec((1, LANES, tb), lambda i, j: (i, 0, j)),
        ],
        out_shape=[
            jax.ShapeDtypeStruct((b, lp, LANES), F32),
            jax.ShapeDtypeStruct((b, lp, LANES), F32),
            jax.ShapeDtypeStruct((b, LANES, lp), F32),
        ],
        scratch_shapes=[pltpu.VMEM((tb, LANES), F32), pltpu.VMEM((LANES, tb), F32)],
        compiler_params=_params(("parallel", "arbitrary")),
        name="logf_cumsum",
    )(fl, b_f)


def _pair_masks(rows):
    lane = lax.broadcasted_iota(jnp.int32, (rows, LANES), 1)
    return lane < HEAD_DIM, lane >= HEAD_DIM


def _bd(x):
    m0, m1 = _pair_masks(x.shape[0])
    zero = jnp.zeros_like(x)
    return jnp.concatenate([jnp.where(m0, x, zero), jnp.where(m1, x, zero)], axis=0)


def _rwkv_pair_chunk(r, k, v, av, b, ld, bds):
    c = r.shape[0]
    ri = lax.broadcasted_iota(jnp.int32, (c, c), 0)
    ci = lax.broadcasted_iota(jnp.int32, (c, c), 1)
    tril = (ci <= ri).astype(BF16)
    ones = jnp.ones((c, LANES), BF16)
    cs = jnp.zeros((c, LANES), F32)
    tot = jnp.zeros((LANES, LANES), F32)
    for part in _split3(ld):
        cs = cs + _dot(tril, part)
        tot = tot + _dot_tn(part, ones)
    e_pos = jnp.exp(cs)
    e_neg = jnp.exp(-cs)
    at = (av * jnp.exp(cs - ld)).astype(BF16)
    rt = (r * e_pos).astype(BF16)
    bt = b * e_neg
    kt = k * e_neg
    e_end = jnp.exp(cs[c - 1:c, :] - cs)
    bh = (b * e_end).astype(BF16)
    kh = (k * e_end).astype(BF16)

    lhs = jnp.concatenate([at, rt], axis=0)
    rhs = jnp.concatenate([_bd(bt), _bd(kt)], axis=0).astype(BF16)
    sc = _dot_nt(lhs, rhs)
    t2 = lax.broadcasted_iota(jnp.int32, (c, 2 * c), 0)
    s2 = lax.broadcasted_iota(jnp.int32, (c, 2 * c), 1) & (c - 1)
    strict = s2 < t2
    incl = s2 <= t2
    a_ab = jnp.where(strict, sc[:c, :2 * c], 0.0)
    a_ak = jnp.where(strict, sc[:c, 2 * c:], 0.0)
    a_rb = jnp.where(incl, sc[c:, :2 * c], 0.0)
    a_rk = jnp.where(incl, sc[c:, 2 * c:], 0.0)

    inv = jnp.where(s2 == t2, 1.0, 0.0) + a_ab
    lp = a_ab
    n_sq = max(c.bit_length() - 2, 0)
    for _ in range(n_sq):
        lp = _dot(lp.astype(BF16), _bd(lp).astype(BF16))
        inv = inv + _dot(inv.astype(BF16), _bd(lp).astype(BF16))

    bds16 = bds.astype(BF16)
    vbd = _bd(v).astype(BF16)
    rhs_u = _dot(at, bds16) + _dot(a_ak.astype(BF16), vbd)
    u = _dot(inv.astype(BF16), _bd(rhs_u).astype(BF16))
    ubd = _bd(u).astype(BF16)
    y = (_dot(rt, bds16)
         + _dot(jnp.concatenate([a_rb, a_rk], axis=1).astype(BF16), jnp.concatenate([ubd, vbd], axis=0)))
    cross = _dot_tn(bh, u.astype(BF16)) + _dot_tn(kh, v.astype(BF16))
    bds_new = jnp.exp(tot) * bds + jnp.where(_same_head(LANES, LANES), cross, 0.0)
    return y, bds_new


def _same_head(rows, cols):
    rr = lax.broadcasted_iota(jnp.int32, (rows, cols), 0) >= HEAD_DIM
    cc = lax.broadcasted_iota(jnp.int32, (rows, cols), 1) >= HEAD_DIM
    return rr == cc


def _head_sum(x, scale=1.0):
    g = jnp.where(_same_head(LANES, LANES), scale, 0.0).astype(BF16)
    hi, lo = _split2(x)
    return _dot(hi, g) + _dot(lo, g)


def _softplus(x):
    return jnp.maximum(x, 0.0) + jnp.log1p(jnp.exp(-jnp.abs(x)))


def _rwkv_prep(ps, w0, w2p, a0, a2p, g2):
    da = w0.shape[1]
    x12 = ps[:, 3 * da:3 * da + LANES]
    lg = ps[:, 3 * da + LANES:3 * da + 2 * LANES]
    w_log = -_softplus(-(w0 + _dot(jnp.tanh(x12).astype(BF16), w2p))) - 0.5
    ld = -jnp.exp(w_log)
    a = jax.nn.sigmoid(a0 + _dot(x12.astype(BF16), a2p))
    g = _dot(jax.nn.sigmoid(lg).astype(BF16), g2)
    return ld, a, g


def _rwkv_pair_inputs(ps, a, kk_w, ka_w, pr):
    da = a.shape[1]
    sl = slice(LANES * pr, LANES * (pr + 1))
    r = ps[:, sl]
    k = ps[:, da + LANES * pr:da + LANES * (pr + 1)]
    v = ps[:, 2 * da + LANES * pr:2 * da + LANES * (pr + 1)]
    kk = k * kk_w[:, sl]
    kk = kk / jnp.maximum(jnp.sqrt(_head_sum(kk * kk)), 1e-12)
    ap = a[:, sl]
    k2 = k * (1.0 + (ap - 1.0) * ka_w[:, sl])
    return r, k2, v, kk, ap


def _rwkv_pair_output(y, r, k2, v, g, rk_w, lnw, lnb, pr):
    sl = slice(LANES * pr, LANES * (pr + 1))
    mean = _head_sum(y, 1.0 / HEAD_DIM)
    d = y - mean
    var = _head_sum(d * d, 1.0 / HEAD_DIM)
    yn = d * lax.rsqrt(var + GN_EPS) * lnw[:, sl] + lnb[:, sl]
    bonus = _head_sum(r * k2 * rk_w[:, sl]) * v
    return (yn + bonus) * g[:, sl]


def _rwkv_mix_kernel(p_ref, mu_ref, w0_ref, w2_ref, a0_ref, a2_ref, g2_ref, kk_ref, ka_ref, rk_ref,
                     lnw_ref, lnb_ref, ya_ref, sfin_ref, prev_ref, st_ref, *, c, n_pairs):
    j = pl.program_id(1)

    @pl.when(j == 0)
    def _():
        prev_ref[...] = jnp.zeros_like(prev_ref)
        st_ref[...] = jnp.zeros_like(st_ref)

    p = p_ref[0]
    row = lax.broadcasted_iota(jnp.int32, (c, 1), 0)
    p_prev = jnp.where(row == 0, prev_ref[0:1, :], pltpu.roll(p, 1, 0))
    prev_ref[...] = jnp.broadcast_to(p[c - 1:c, :], prev_ref.shape)
    ps = p + (p_prev - p) * mu_ref[...]
    ld, a, g = _rwkv_prep(ps, w0_ref[...], w2_ref[...], a0_ref[...], a2_ref[...], g2_ref[...])
    for pr in range(n_pairs):
        sl = slice(LANES * pr, LANES * (pr + 1))
        r, k2, v, kk, ap = _rwkv_pair_inputs(ps, a, kk_ref[...], ka_ref[...], pr)
        y, bds = _rwkv_pair_chunk(r, k2, v, -kk, kk * ap, ld[:, sl], st_ref[pr])
        st_ref[pr] = bds
        out = _rwkv_pair_output(y, r, k2, v, g, rk_ref[...], lnw_ref[...], lnb_ref[...], pr)
        ya_ref[0, :, sl] = out.astype(ya_ref.dtype)

    @pl.when(j == pl.num_programs(1) - 1)
    def _():
        sfin_ref[0] = st_ref[...]


def rwkv_mix_prompt(p, mu, w0, w2p, a0, a2p, g2, kk_w, ka_w, rk_w, lnw, lnb):
    b, lp, ds = p.shape
    da = w0.shape[1]
    n_pairs = da // LANES
    c = RWKV_CHUNK
    vec = lambda n: pl.BlockSpec((1, n), lambda i, j: (0, 0))
    mat = lambda w: pl.BlockSpec(w.shape, lambda i, j: (0, 0))
    return pl.pallas_call(
        functools.partial(_rwkv_mix_kernel, c=c, n_pairs=n_pairs),
        grid=(b, lp // c),
        in_specs=[pl.BlockSpec((1, c, ds), lambda i, j: (i, j, 0)), vec(ds), vec(da), mat(w2p), vec(da), mat(a2p),
                  mat(g2), vec(da), vec(da), vec(da), vec(da), vec(da)],
        out_specs=[pl.BlockSpec((1, c, da), lambda i, j: (i, j, 0)),
                   pl.BlockSpec((1, n_pairs, LANES, LANES), lambda i, j: (i, 0, 0, 0))],
        out_shape=[jax.ShapeDtypeStruct((b, lp, da), BF16),
                   jax.ShapeDtypeStruct((b, n_pairs, LANES, LANES), F32)],
        scratch_shapes=[pltpu.VMEM((8, ds), F32), pltpu.VMEM((n_pairs, LANES, LANES), F32)],
        compiler_params=_params(("parallel", "arbitrary")),
        name="rwkv_mix_prompt",
    )(p, mu, w0, w2p, a0, a2p, g2, kk_w, ka_w, rk_w, lnw, lnb)


def _rwkv_step_kernel(p_ref, prev_ref, s_ref, mu_ref, w0_ref, w2_ref, a0_ref, a2_ref, g2_ref, kk_ref, ka_ref,
                      rk_ref, lnw_ref, lnb_ref, ya_ref, snew_ref, r_s, w_s, k_s, v_s, a_s, b_s, y_s,
                      *, n_pairs):
    p = p_ref[...]
    ps = p + (prev_ref[...] - p) * mu_ref[...]
    ld, a, g = _rwkv_prep(ps, w0_ref[...], w2_ref[...], a0_ref[...], a2_ref[...], g2_ref[...])
    pair_vals = []
    for pr in range(n_pairs):
        sl = slice(LANES * pr, LANES * (pr + 1))
        r, k2, v, kk, ap = _rwkv_pair_inputs(ps, a, kk_ref[...], ka_ref[...], pr)
        pair_vals.append((r, k2, v))
        r_s[:, sl] = r
        w_s[:, sl] = jnp.exp(ld[:, sl])
        k_s[:, sl] = k2
        v_s[:, sl] = v
        a_s[:, sl] = -kk
        b_s[:, sl] = kk * ap

    hd = HEAD_DIM
    eye = lax.broadcasted_iota(jnp.int32, (hd, hd), 0) == lax.broadcasted_iota(jnp.int32, (hd, hd), 1)
    sel = (lax.broadcasted_iota(jnp.int32, (hd, LANES), 1) & (hd - 1)) == lax.broadcasted_iota(
        jnp.int32, (hd, LANES), 0)
    lane = lax.broadcasted_iota(jnp.int32, (1, LANES), 1)

    def body(bg, carry):
        b0 = pl.multiple_of(bg * 8, 8)
        for pr in range(n_pairs):
            sl = slice(LANES * pr, LANES * (pr + 1))
            tiles = {}
            for name, ref in (("a", a_s), ("b", b_s), ("w", w_s), ("k", k_s), ("r", r_s), ("v", v_s)):
                x = ref[pl.ds(b0, 8), sl]
                tiles[name] = (x, pltpu.roll(x, hd, 1))
            y_rows = []
            for i in range(8):
                heads = []
                for hh in range(2):
                    vec = lambda name: tiles[name][hh][i:i + 1, :hd]
                    s = s_ref[b0 + i, 2 * pr + hh]
                    sa = jnp.sum(s * vec("a"), axis=1, keepdims=True)
                    v_col = jnp.sum(jnp.where(eye, vec("v"), 0.0), axis=1, keepdims=True)
                    s2 = s * vec("w") + sa * vec("b") + v_col * vec("k")
                    snew_ref[b0 + i, 2 * pr + hh] = s2
                    y_col = jnp.sum(s2 * vec("r"), axis=1, keepdims=True)
                    heads.append(jnp.sum(jnp.where(sel, y_col, 0.0), axis=0, keepdims=True))
                y_rows.append(jnp.where(lane < hd, heads[0], heads[1]))
            y_s[pl.ds(b0, 8), sl] = jnp.concatenate(y_rows, axis=0)
        return carry

    lax.fori_loop(0, p.shape[0] // 8, body, 0)
    y = y_s[...]
    for pr in range(n_pairs):
        sl = slice(LANES * pr, LANES * (pr + 1))
        r, k2, v = pair_vals[pr]
        out = _rwkv_pair_output(y[:, sl], r, k2, v, g, rk_ref[...], lnw_ref[...], lnb_ref[...], pr)
        ya_ref[:, sl] = out.astype(ya_ref.dtype)


def rwkv_step(p, prev, state, mu, w0, w2p, a0, a2p, g2, kk_w, ka_w, rk_w, lnw, lnb):
    db, ds = p.shape
    da = w0.shape[1]
    return pl.pallas_call(
        functools.partial(_rwkv_step_kernel, n_pairs=da // LANES),
        out_shape=[jax.ShapeDtypeStruct((db, da), BF16), jax.ShapeDtypeStruct(state.shape, F32)],
        scratch_shapes=[pltpu.VMEM((db, da), F32) for _ in range(7)],
        compiler_params=pltpu.CompilerParams(vmem_limit_bytes=VMEM_LIMIT),
        name="rwkv_step",
    )(p, prev, state, mu, w0, w2p, a0, a2p, g2, kk_w, ka_w, rk_w, lnw, lnb)


def _head_rows(q_row, rows):
    width = q_row.shape[1]
    row = lax.broadcasted_iota(jnp.int32, (rows, width), 0)
    col = lax.broadcasted_iota(jnp.int32, (rows, width), 1) >> 6
    return jnp.where(row == col, q_row, 0.0)


def _diag_blocks(acc):
    row = lax.broadcasted_iota(jnp.int32, acc.shape, 0)
    col = lax.broadcasted_iota(jnp.int32, acc.shape, 1) >> 6
    return jnp.sum(jnp.where(row == col, acc, 0.0), axis=0, keepdims=True)


def _log_sigmoid(x):
    return jnp.minimum(x, 0.0) - jnp.log1p(jnp.exp(-jnp.abs(x)))


def _sb_decode_kernel(pt_ref, q_ref, kn_ref, vn_ref, u2_ref, *rest, pp, n_pages, n_heads):
    k_refs, v_refs = rest[:pp], rest[pp:2 * pp]
    o_ref, qt_ref, acc_ref, run_ref = rest[2 * pp:]
    j = pl.program_id(1)
    qpos = n_pages * PAGE_SIZE
    lane = lax.broadcasted_iota(jnp.int32, (1, PAGE_SIZE), 1)

    def page(kpg, vpg, kpos0, n_valid):
        z = _dot_nt(qt_ref[...], kpg.astype(BF16))[:n_heads]
        vis = ((kpos0 + lane) < qpos) & (lane < n_valid)
        ls = _log_sigmoid(z)
        lr = jnp.where(vis, ls - z, 0.0)
        hi, lo = _split2(lr)
        s2 = _dot(jnp.concatenate([hi, lo], axis=1), u2_ref[...])
        a = jnp.where(vis, jnp.exp(ls + s2[:, :PAGE_SIZE] + run_ref[...]), 0.0)
        acc_ref[...] += _dot(a.astype(BF16), vpg.astype(BF16))
        run_ref[...] += s2[:, PAGE_SIZE:]

    @pl.when(j == 0)
    def _():
        qt_ref[...] = _head_rows(q_ref[0].astype(F32), qt_ref.shape[0]).astype(BF16)
        acc_ref[...] = jnp.zeros_like(acc_ref)
        run_ref[...] = jnp.zeros_like(run_ref)
        page(kn_ref[0], vn_ref[0], qpos, 1)

    for s in range(pp):
        page(k_refs[s][0], v_refs[s][0], (n_pages - 1 - (j * pp + s)) * PAGE_SIZE, PAGE_SIZE)

    @pl.when(j == pl.num_programs(1) - 1)
    def _():
        o_ref[0] = _diag_blocks(acc_ref[...]).astype(o_ref.dtype)


def _page_specs(pp, n_pages, width):
    def spec(s):
        return pl.BlockSpec((1, PAGE_SIZE, width), lambda b, j, pt: (pt[b, n_pages - 1 - (j * pp + s)], 0, 0))
    return [spec(s) for s in range(pp)]


def sb_decode(q, k_new, v_new, cache_k, cache_v, page_table, *, pp):
    db, _, d = q.shape
    n_pages = page_table.shape[1]
    n_heads = d // HEAD_DIM
    u2 = _suffix_matrix(PAGE_SIZE)
    row = lambda r, w: pl.BlockSpec((1, r, w), lambda b, j, pt: (b, 0, 0))
    grid_spec = pltpu.PrefetchScalarGridSpec(
        num_scalar_prefetch=1,
        grid=(db, n_pages // pp),
        in_specs=[row(1, d), row(PAGE_SIZE, d), row(PAGE_SIZE, d),
                  pl.BlockSpec(u2.shape, lambda b, j, pt: (0, 0))]
                 + _page_specs(pp, n_pages, d) + _page_specs(pp, n_pages, d),
        out_specs=row(1, d),
        scratch_shapes=[pltpu.VMEM((16, d), BF16), pltpu.VMEM((n_heads, d), F32),
                        pltpu.VMEM((n_heads, PAGE_SIZE), F32)],
    )
    return pl.pallas_call(
        functools.partial(_sb_decode_kernel, pp=pp, n_pages=n_pages, n_heads=n_heads),
        grid_spec=grid_spec,
        out_shape=jax.ShapeDtypeStruct((db, 1, d), BF16),
        compiler_params=_params(("parallel", "arbitrary")),
        name="sb_decode",
    )(page_table, q, k_new, v_new, u2, *([cache_k] * pp), *([cache_v] * pp))


def _fox_decode_kernel(pt_ref, q_ref, kn_ref, vn_ref, fl_ref, bf_ref, m2_ref, *rest, pp, n_pages, n_heads):
    k_refs, v_refs, lf_refs = rest[:pp], rest[pp:2 * pp], rest[2 * pp:3 * pp]
    o_ref, lfo_ref, qt_ref, acc_ref, m_ref, l_ref, run_ref = rest[3 * pp:]
    j = pl.program_id(1)
    qpos = n_pages * PAGE_SIZE
    lane = lax.broadcasted_iota(jnp.int32, (1, PAGE_SIZE), 1)

    def page(kpg, vpg, bias, kpos0, n_valid):
        z = _dot_nt(qt_ref[...], kpg.astype(BF16))[:n_heads] + bias
        vis = ((kpos0 + lane) <= qpos) & (lane < n_valid)
        z = jnp.where(vis, z, NEG_INF)
        m_new = jnp.maximum(m_ref[...], jnp.max(z, axis=1, keepdims=True))
        alpha = jnp.exp(m_ref[...] - m_new)
        p = jnp.exp(z - m_new)
        l_ref[...] = alpha * l_ref[...] + jnp.sum(p, axis=1, keepdims=True)
        acc_ref[...] = alpha * acc_ref[...] + _dot(p.astype(BF16), vpg.astype(BF16))
        m_ref[...] = m_new

    @pl.when(j == 0)
    def _():
        qt_ref[...] = _head_rows(q_ref[0].astype(F32), qt_ref.shape[0]).astype(BF16)
        acc_ref[...] = jnp.zeros_like(acc_ref)
        m_ref[...] = jnp.full_like(m_ref, NEG_INF)
        l_ref[...] = jnp.zeros_like(l_ref)
        logf = _log_sigmoid(fl_ref[0] + bf_ref[...])
        lfo_ref[0] = logf
        eye = lax.broadcasted_iota(jnp.int32, (LANES, LANES), 0) == lax.broadcasted_iota(
            jnp.int32, (LANES, LANES), 1)
        col = jnp.sum(jnp.where(eye, logf, 0.0), axis=1, keepdims=True)
        run_ref[...] = jnp.broadcast_to(col[:n_heads], run_ref.shape)
        page(kn_ref[0], vn_ref[0], 0.0, qpos, 1)

    for s in range(pp):
        lf = lf_refs[s][0]
        s2 = jnp.zeros((n_heads, 2 * PAGE_SIZE), F32)
        for part in _split3(lf):
            s2 = s2 + _dot(part, m2_ref[...])
        page(k_refs[s][0], v_refs[s][0], run_ref[...] + s2[:, :PAGE_SIZE],
             (n_pages - 1 - (j * pp + s)) * PAGE_SIZE, PAGE_SIZE)
        run_ref[...] += s2[:, PAGE_SIZE:]

    @pl.when(j == pl.num_programs(1) - 1)
    def _():
        o_ref[0] = _diag_blocks(acc_ref[...] / l_ref[...]).astype(o_ref.dtype)


def fox_decode(q, k_new, v_new, fl_new, b_f, cache_k, cache_v, cache_lf_t, page_table, *, pp):
    db, _, d = q.shape
    n_pages = page_table.shape[1]
    n_heads = d // HEAD_DIM
    jj = jnp.arange(PAGE_SIZE)[:, None]
    ss = jnp.arange(PAGE_SIZE)[None, :]
    m2 = jnp.concatenate([(jj > ss).astype(BF16), jnp.ones((PAGE_SIZE, PAGE_SIZE), BF16)], axis=1)
    row = lambda r, w: pl.BlockSpec((1, r, w), lambda b, j, pt: (b, 0, 0))
    lf_specs = [pl.BlockSpec((1, n_heads, PAGE_SIZE),
                             lambda b, j, pt, s=s: (pt[b, n_pages - 1 - (j * pp + s)], 0, 0)) for s in range(pp)]
    grid_spec = pltpu.PrefetchScalarGridSpec(
        num_scalar_prefetch=1,
        grid=(db, n_pages // pp),
        in_specs=[row(1, d), row(PAGE_SIZE, d), row(PAGE_SIZE, d), row(1, LANES),
                  pl.BlockSpec((1, LANES), lambda b, j, pt: (0, 0)),
                  pl.BlockSpec(m2.shape, lambda b, j, pt: (0, 0))]
                 + _page_specs(pp, n_pages, d) + _page_specs(pp, n_pages, d) + lf_specs,
        out_specs=[row(1, d), row(1, LANES)],
        scratch_shapes=[pltpu.VMEM((16, d), BF16), pltpu.VMEM((n_heads, d), F32),
                        pltpu.VMEM((n_heads, 1), F32), pltpu.VMEM((n_heads, 1), F32),
                        pltpu.VMEM((n_heads, PAGE_SIZE), F32)],
    )
    return pl.pallas_call(
        functools.partial(_fox_decode_kernel, pp=pp, n_pages=n_pages, n_heads=n_heads),
        grid_spec=grid_spec,
        out_shape=[jax.ShapeDtypeStruct((db, 1, d), BF16), jax.ShapeDtypeStruct((db, 1, LANES), F32)],
        compiler_params=_params(("parallel", "arbitrary")),
        name="fox_decode",
    )(page_table, q, k_new, v_new, fl_new, b_f, m2, *([cache_k] * pp), *([cache_v] * pp), *([cache_lf_t] * pp))


def _pad_rows(x, rows):
    return jnp.pad(x, ((0, 0), (0, rows - x.shape[1]), (0, 0)))


def _pad_lanes(x):
    return jnp.pad(x, [(0, 0)] * (x.ndim - 1) + [(0, LANES - x.shape[-1])])


def kernel(x_prompt, x_sample, cache_sb_k, cache_sb_v, cache_fox_k, cache_fox_v, cache_fox_logf, state_rwkv, state_rwkv_shift, page_table, meta_tokens, norm_g, ffn_w1, ffn_w3, ffn_w2, w_in_even, w_out_even, rwkv_mu, rwkv_w0, rwkv_w2, rwkv_a0, rwkv_a2, rwkv_g2, rwkv_kk, rwkv_ka, rwkv_rk, rwkv_lnw, rwkv_lnb, w_in_odd, b_f, w_out_odd):
    bsz, seq, d = x_prompt.shape
    db, ts, _ = x_sample.shape
    assert ts == 1, "the decode kernels handle one new token per sequence"
    depth = norm_g.shape[0]
    t_p = N_META + seq
    lp = LEAD + t_p
    n_pool, n_pages = cache_sb_k.shape[1], page_table.shape[1]
    h_a = state_rwkv.shape[2]
    d_a = h_a * HEAD_DIM
    d_b = cache_sb_k.shape[3] * HEAD_DIM
    d_c = cache_fox_k.shape[3] * HEAD_DIM
    h_c = d_c // HEAD_DIM
    d_shift = state_rwkv_shift.shape[2]
    dff = ffn_w1.shape[3]

    hp = jnp.concatenate([jnp.zeros((bsz, LEAD, d), F32),
                          jnp.broadcast_to(meta_tokens[None].astype(F32), (bsz, N_META, d)), x_prompt], axis=1)
    hs = x_sample.reshape(db, d)

    tm = _row_tile(lp)
    tq = tm
    tf = dff // 2 if (dff // 2) % LANES == 0 else dff
    pp = min(8, n_pages)
    gvec = lambda l, i: norm_g[l, i][None, :]

    def ffn(h2, l, i):
        w1, w3, w2 = (w[l, i].astype(BF16) for w in (ffn_w1, ffn_w3, ffn_w2))
        return ffn_half(h2, gvec(l, 2 * i * 2), gvec(l, 2 * i * 2 + 1), w1, w3, w2,
                        tm=_row_tile(h2.shape[0]), tf=tf)

    outs_p = {k: [] for k in ("sb_k", "sb_v", "fk", "fv", "fl", "rw", "sh")}
    outs_s = {k: [] for k in ("sb_k", "sb_v", "fk", "fv", "fl", "rw", "sh")}

    for l in range(depth):
        hp = ffn(hp.reshape(bsz * lp, d), l, 0).reshape(bsz, lp, d)
        hs = ffn(hs, l, 0)
        g_mix, g_post = gvec(l, 2), gvec(l, 3)
        if l % 2 == 0:
            e = l // 2
            w_in = w_in_even[e].astype(BF16)
            slabs = [w_in[:, :d_shift], w_in[:, d_shift:d_shift + d_b],
                     w_in[:, d_shift + d_b:d_shift + 2 * d_b], w_in[:, d_shift + 2 * d_b:]]
            dts = [(F32,), (BF16,), (F32, BF16), (F32, BF16)]
            scl = [1.0, HEAD_DIM ** -0.5, 1.0, 1.0]
            zeros_l = jnp.zeros((LORA_W, d_a), F32)
            rw = (rwkv_mu[e][None], rwkv_w0[e][None],
                  jnp.concatenate([rwkv_w2[e], zeros_l], axis=0).astype(BF16), rwkv_a0[e][None],
                  jnp.concatenate([zeros_l, rwkv_a2[e]], axis=0).astype(BF16), rwkv_g2[e].astype(BF16),
                  rwkv_kk[e][None], rwkv_ka[e][None], rwkv_rk[e].reshape(1, d_a), rwkv_lnw[e][None],
                  rwkv_lnb[e][None])
            w_out = w_out_even[e].astype(BF16)
            p, q, k, kb, v, vb = norm_proj(hp.reshape(bsz * lp, d), g_mix, slabs, dts, scl, tm=tm)
            r3 = lambda x: x.reshape(bsz, lp, x.shape[-1])
            ya, sfin = rwkv_mix_prompt(r3(p), *rw)
            yb = sb_attention(r3(q), r3(kb), r3(vb), tq=tq, lead=LEAD)
            hp = out_proj(hp, g_post, [ya, yb], [w_out[:d_a], w_out[d_a:]], tm=tm, lead=LEAD)
            outs_p["sb_k"].append(r3(k)[:, LEAD:].reshape(bsz, t_p, -1, HEAD_DIM))
            outs_p["sb_v"].append(r3(v)[:, LEAD:].reshape(bsz, t_p, -1, HEAD_DIM))
            s_pairs = sfin.reshape(bsz, d_a // LANES, 2, HEAD_DIM, 2, HEAD_DIM)
            s_heads = jnp.stack([s_pairs[:, :, 0, :, 0, :], s_pairs[:, :, 1, :, 1, :]], axis=2)
            outs_p["rw"].append(jnp.swapaxes(s_heads.reshape(bsz, h_a, HEAD_DIM, HEAD_DIM), -1, -2))
            outs_p["sh"].append(r3(p)[:, -1])
            p, q, k, kb, v, vb = norm_proj(hs, g_mix, slabs, dts, scl, tm=_row_tile(db))
            ya, s_new = rwkv_step(p, state_rwkv_shift[e], state_rwkv[e], *rw)
            yb = sb_decode(q[:, None], _pad_rows(k[:, None], PAGE_SIZE), _pad_rows(v[:, None], PAGE_SIZE),
                           cache_sb_k[e].reshape(n_pool, PAGE_SIZE, d_b),
                           cache_sb_v[e].reshape(n_pool, PAGE_SIZE, d_b), page_table, pp=pp)
            hs = out_proj(hs[None], g_post, [ya[None], yb.reshape(1, db, d_b)], [w_out[:d_a], w_out[d_a:]],
                          tm=_row_tile(db), lead=0)[0]
            outs_s["sb_k"].append(k.reshape(db, 1, -1, HEAD_DIM))
            outs_s["sb_v"].append(v.reshape(db, 1, -1, HEAD_DIM))
            outs_s["rw"].append(s_new)
            outs_s["sh"].append(p)
        else:
            o = l // 2
            w_in = w_in_odd[o]
            slabs = [w_in[:, :d_c].astype(BF16), w_in[:, d_c:2 * d_c].astype(BF16),
                     w_in[:, 2 * d_c:3 * d_c].astype(BF16), _pad_lanes(w_in[:, 3 * d_c:]).astype(BF16)]
            dts = [(BF16,), (F32, BF16), (F32, BF16), (F32,)]
            scl = [HEAD_DIM ** -0.5, 1.0, 1.0, 1.0]
            bf_row = _pad_lanes(b_f[o][None])
            w_out = w_out_odd[o].astype(BF16)
            q, k, kb, v, vb, fl = norm_proj(hp.reshape(bsz * lp, d), g_mix, slabs, dts, scl, tm=tm)
            r3 = lambda x: x.reshape(bsz, lp, x.shape[-1])
            logf, cum, cum_t = logf_cumsum(r3(fl), bf_row, lead=LEAD)
            yc = fox_attention(r3(q), r3(kb), r3(vb), cum, cum_t[:, :h_c, None, :], tq=tq, lead=LEAD)
            hp = out_proj(hp, g_post, [yc], [w_out], tm=tm, lead=LEAD)
            outs_p["fk"].append(r3(k)[:, LEAD:].reshape(bsz, t_p, h_c, HEAD_DIM))
            outs_p["fv"].append(r3(v)[:, LEAD:].reshape(bsz, t_p, h_c, HEAD_DIM))
            outs_p["fl"].append(logf[:, LEAD:, :h_c])
            q, k, kb, v, vb, fl = norm_proj(hs, g_mix, slabs, dts, scl, tm=_row_tile(db))
            yc, logf_s = fox_decode(q[:, None], _pad_rows(k[:, None], PAGE_SIZE), _pad_rows(v[:, None], PAGE_SIZE),
                                    fl[:, None], bf_row, cache_fox_k[o].reshape(n_pool, PAGE_SIZE, d_c),
                                    cache_fox_v[o].reshape(n_pool, PAGE_SIZE, d_c),
                                    jnp.swapaxes(cache_fox_logf[o], 1, 2), page_table, pp=pp)
            hs = out_proj(hs[None], g_post, [yc.reshape(1, db, d_c)], [w_out], tm=_row_tile(db), lead=0)[0]
            outs_s["fk"].append(k.reshape(db, 1, h_c, HEAD_DIM))
            outs_s["fv"].append(v.reshape(db, 1, h_c, HEAD_DIM))
            outs_s["fl"].append(logf_s[:, :, :h_c])
        hp = ffn(hp.reshape(bsz * lp, d), l, 1).reshape(bsz, lp, d)
        hs = ffn(hs, l, 1)

    y_prompt = hp[:, LEAD + N_META:]
    y_sample = hs.reshape(db, 1, d)
    order = ("sb_k", "sb_v", "fk", "fv", "fl", "rw", "sh")
    return (y_prompt, y_sample) + tuple(jnp.stack(outs_p[k]) for k in order) + tuple(
        jnp.stack(outs_s[k]) for k in order)
```

```python
import functools

import jax
import jax.numpy as jnp
from jax import lax
from jax.experimental import pallas as pl
from jax.experimental.pallas import tpu as pltpu

F32 = jnp.float32
BF16 = jnp.bfloat16

HEAD_DIM = 64
N_META = 16
ATTN_BLOCK = 128
LEAD = (-N_META) % ATTN_BLOCK
PAGE_SIZE = 128
LORA_W = 64
LORA_A = 64
LORA_G = 128
FFN_RES = 0.5
NORM_EPS = 1e-6
GN_EPS = 64e-5
NEG_INF = -1e30
LANES = 128
RWKV_CHUNK = 64
VMEM_LIMIT = 52 * 1024 * 1024


def _params(sem):
    return pltpu.CompilerParams(dimension_semantics=sem, vmem_limit_bytes=VMEM_LIMIT)


def _row_tile(m, candidates=(384, 256, 128, 64, 32, 16, 8)):
    for t in candidates:
        if m % t == 0:
            return t
    raise ValueError(f"no row tile for {m}")


def _rms(x, g):
    return x * lax.rsqrt(jnp.mean(x * x, axis=-1, keepdims=True) + NORM_EPS) * g


def _split3(x):
    h = x.astype(BF16)
    r = x - h.astype(F32)
    m = r.astype(BF16)
    l = (r - m.astype(F32)).astype(BF16)
    return h, m, l


def _split2(x):
    h = x.astype(BF16)
    return h, (x - h.astype(F32)).astype(BF16)


def _dot(a, b):
    return jnp.dot(a, b, preferred_element_type=F32)


def _dot_nt(a, b):
    return lax.dot_general(a, b, (((1,), (1,)), ((), ())), preferred_element_type=F32)


def _dot_tn(a, b):
    return lax.dot_general(a, b, (((0,), (0,)), ((), ())), preferred_element_type=F32)


def _ffn_kernel(h_ref, gpre_ref, gpost_ref, w1_ref, w3_ref, w2_ref, o_ref, xn_ref, acc_ref):
    j = pl.program_id(1)

    @pl.when(j == 0)
    def _():
        xn_ref[...] = _rms(h_ref[...], gpre_ref[...]).astype(BF16)
        acc_ref[...] = jnp.zeros_like(acc_ref)

    x = xn_ref[...]
    a = _dot(x, w1_ref[...])
    b = _dot(x, w3_ref[...])
    t = (a * jax.nn.sigmoid(a) * b).astype(BF16)
    acc_ref[...] += _dot(t, w2_ref[...])

    @pl.when(j == pl.num_programs(1) - 1)
    def _():
        o_ref[...] = h_ref[...] + FFN_RES * _rms(acc_ref[...], gpost_ref[...])


def ffn_half(h, g_pre, g_post, w1, w3, w2, *, tm, tf):
    m, d = h.shape
    dff = w1.shape[1]
    return pl.pallas_call(
        _ffn_kernel,
        grid=(m // tm, dff // tf),
        in_specs=[
            pl.BlockSpec((tm, d), lambda i, j: (i, 0)),
            pl.BlockSpec((1, d), lambda i, j: (0, 0)),
            pl.BlockSpec((1, d), lambda i, j: (0, 0)),
            pl.BlockSpec((d, tf), lambda i, j: (0, j)),
            pl.BlockSpec((d, tf), lambda i, j: (0, j)),
            pl.BlockSpec((tf, d), lambda i, j: (j, 0)),
        ],
        out_specs=pl.BlockSpec((tm, d), lambda i, j: (i, 0)),
        out_shape=jax.ShapeDtypeStruct((m, d), F32),
        scratch_shapes=[pltpu.VMEM((tm, d), BF16), pltpu.VMEM((tm, d), F32)],
        compiler_params=_params(("parallel", "arbitrary")),
        name="ffn_half",
    )(h, g_pre, g_post, w1, w3, w2)


def _proj_kernel(*refs, out_dtypes, scales):
    h_ref, g_ref = refs[0], refs[1]
    n_w = len(out_dtypes)
    w_refs = refs[2:2 + n_w]
    o_refs = iter(refs[2 + n_w:])
    x = _rms(h_ref[...], g_ref[...]).astype(BF16)
    for w_ref, dts, s in zip(w_refs, out_dtypes, scales):
        y = _dot(x, w_ref[...])
        if s != 1.0:
            y = y * s
        for dt in dts:
            next(o_refs)[...] = y.astype(dt)


def norm_proj(h, g, weights, out_dtypes, scales, *, tm):
    m, d = h.shape
    in_specs = [pl.BlockSpec((tm, d), lambda i: (i, 0)), pl.BlockSpec((1, d), lambda i: (0, 0))]
    in_specs += [pl.BlockSpec(w.shape, lambda i: (0, 0)) for w in weights]
    outs = [(w.shape[1], dt) for w, dts in zip(weights, out_dtypes) for dt in dts]
    return pl.pallas_call(
        functools.partial(_proj_kernel, out_dtypes=tuple(tuple(d) for d in out_dtypes), scales=tuple(scales)),
        grid=(m // tm,),
        in_specs=in_specs,
        out_specs=[pl.BlockSpec((tm, n), lambda i: (i, 0)) for n, _ in outs],
        out_shape=[jax.ShapeDtypeStruct((m, n), dt) for n, dt in outs],
        compiler_params=_params(("parallel",)),
        name="norm_proj",
    )(h, g, *weights)


def _outproj_kernel(*refs, n_in, tm, lead):
    h_ref, g_ref = refs[0], refs[1]
    y_refs = refs[2:2 + n_in]
    w_refs = refs[2 + n_in:2 + 2 * n_in]
    o_ref = refs[2 + 2 * n_in]
    m = _dot(y_refs[0][0], w_refs[0][...])
    for y_ref, w_ref in zip(y_refs[1:], w_refs[1:]):
        m = m + _dot(y_ref[0], w_ref[...])
    out = h_ref[0] + _rms(m, g_ref[...])
    pos = pl.program_id(1) * tm + lax.broadcasted_iota(jnp.int32, (tm, 1), 0)
    o_ref[0] = jnp.where(pos >= lead, out, 0.0)


def out_proj(h, g, ys, ws, *, tm, lead):
    b, lp, d = h.shape
    n = len(ys)
    in_specs = [pl.BlockSpec((1, tm, d), lambda i, j: (i, j, 0)), pl.BlockSpec((1, d), lambda i, j: (0, 0))]
    in_specs += [pl.BlockSpec((1, tm, y.shape[2]), lambda i, j: (i, j, 0)) for y in ys]
    in_specs += [pl.BlockSpec(w.shape, lambda i, j: (0, 0)) for w in ws]
    return pl.pallas_call(
        functools.partial(_outproj_kernel, n_in=n, tm=tm, lead=lead),
        grid=(b, lp // tm),
        in_specs=in_specs,
        out_specs=pl.BlockSpec((1, tm, d), lambda i, j: (i, j, 0)),
        out_shape=jax.ShapeDtypeStruct((b, lp, d), F32),
        compiler_params=_params(("parallel", "parallel")),
        name="out_proj",
    )(h, g, *ys, *ws)


def _sb_attn_kernel(q_ref, k_ref, v_ref, u2_ref, o_ref, *, tq, ts, lead):
    q0 = pl.program_id(2) * tq
    lane = lax.broadcasted_iota(jnp.int32, (1, LANES), 1)
    head0 = lane < HEAD_DIM
    rel = lax.broadcasted_iota(jnp.int32, (ts, ts), 1) - lax.broadcasted_iota(jnp.int32, (ts, ts), 0)
    kcol = lax.broadcasted_iota(jnp.int32, (ts, ts), 1)

    def sub_tile(st, _):
        r0 = pl.multiple_of(st * ts, ts)
        qs0 = q0 + r0
        diag = qs0 // ts
        q = q_ref[0, pl.ds(r0, ts), :]
        zq = jnp.zeros_like(q)
        q2 = jnp.concatenate([jnp.where(head0, q, zq), jnp.where(head0, zq, q)], axis=0)
        edge = lambda ks: (rel < (qs0 - ks)) & ((ks + kcol) >= lead)

        def step(ks, carry, nb, masked):
            acc, run = carry
            zs = [_dot_nt(q2, k_ref[0, pl.ds(_aligned(ks + g * ts, ts), min(2, nb - g) * ts), :])
                  for g in range(0, nb, 2)]
            a_blocks = [None] * nb
            for kk in reversed(range(nb)):
                zz = zs[kk // 2][:, (kk % 2) * ts:(kk % 2 + 1) * ts]
                ls = _log_sigmoid(zz)
                lr = ls - zz
                if masked:
                    vis = edge(ks + kk * ts)
                    vis2 = jnp.concatenate([vis, vis], axis=0)
                    lr = jnp.where(vis2, lr, 0.0)
                hi, lo = _split2(lr)
                s2 = _dot(jnp.concatenate([hi, lo], axis=1), u2_ref[...])
                a = jnp.exp(ls + s2[:, :ts] + run)
                if masked:
                    a = jnp.where(vis2, a, 0.0)
                a_blocks[kk] = a.astype(BF16)
                run = run + s2[:, ts:]
            for g in range(0, nb, 2):
                grp = a_blocks[g:g + 2]
                a2 = grp[0] if len(grp) == 1 else jnp.concatenate(grp, axis=1)
                acc = acc + _dot(a2, v_ref[0, pl.ds(_aligned(ks + g * ts, ts), len(grp) * ts), :])
            return acc, run

        zero = jnp.zeros((2 * ts, LANES), F32)
        blk = lambda kb: pl.multiple_of(kb * ts, ts)
        carry = step(blk(diag), (zero, zero), 1, True)
        n_mid = jnp.maximum(diag - 1, 0)
        n4 = lax.shift_right_logical(n_mid, 2)
        carry = lax.fori_loop(0, n4, lambda j, c: step(blk(diag - 4 - 4 * j), c, 4, False), carry)
        carry = _loop_if((n_mid & 2) != 0, lambda c: step(blk(1 + (n_mid & 1)), c, 2, False), carry)
        carry = _loop_if((n_mid & 1) != 0, lambda c: step(blk(1), c, 1, False), carry)
        carry = _loop_if(diag >= 1, lambda c: step(0, c, 1, True), carry)
        acc = carry[0]
        o_ref[0, pl.ds(r0, ts), :] = jnp.where(head0, acc[:ts], acc[ts:]).astype(o_ref.dtype)
        return 0

    lax.fori_loop(0, tq // ts, sub_tile, 0)


def _aligned(x, m):
    return x if isinstance(x, int) else pl.multiple_of(x, m)


def _loop_if(pred, f, carry):
    return lax.fori_loop(0, pred.astype(jnp.int32), lambda j, c: f(c), carry)


def _suffix_matrix(tk):
    j = jnp.arange(tk)[:, None]
    s = jnp.arange(tk)[None, :]
    u = jnp.concatenate([(j > s).astype(BF16), jnp.ones((tk, tk), BF16)], axis=1)
    return jnp.concatenate([u, u], axis=0)


def sb_attention(q, k, v, *, tq, lead):
    b, lp, dq = q.shape
    ts = ATTN_BLOCK
    u2 = _suffix_matrix(ts)
    return pl.pallas_call(
        functools.partial(_sb_attn_kernel, tq=tq, ts=ts, lead=lead),
        grid=(b, dq // LANES, lp // tq),
        in_specs=[
            pl.BlockSpec((1, tq, LANES), lambda i, p, j: (i, j, p)),
            pl.BlockSpec((1, lp, LANES), lambda i, p, j: (i, 0, p)),
            pl.BlockSpec((1, lp, LANES), lambda i, p, j: (i, 0, p)),
            pl.BlockSpec(u2.shape, lambda i, p, j: (0, 0)),
        ],
        out_specs=pl.BlockSpec((1, tq, LANES), lambda i, p, j: (i, j, p)),
        out_shape=jax.ShapeDtypeStruct((b, lp, dq), BF16),
        compiler_params=_params(("parallel", "parallel", "arbitrary")),
        name="sb_attention",
    )(q, k, v, u2)


def _fox_attn_kernel(q_ref, k_ref, v_ref, cq_ref, ckt_ref, o_ref, *, tq, ts, lead):
    pair = pl.program_id(1)
    q0 = pl.program_id(2) * tq
    lane = lax.broadcasted_iota(jnp.int32, (1, LANES), 1)
    head0 = lane < HEAD_DIM
    rel = lax.broadcasted_iota(jnp.int32, (ts, ts), 1) - lax.broadcasted_iota(jnp.int32, (ts, ts), 0)
    kcol = lax.broadcasted_iota(jnp.int32, (ts, ts), 1)

    def sub_tile(st, _):
        r0 = pl.multiple_of(st * ts, ts)
        qs0 = q0 + r0
        diag = qs0 // ts
        q = q_ref[0, pl.ds(r0, ts), :]
        cq_all = cq_ref[0, pl.ds(r0, ts), :]
        zq = jnp.zeros_like(q)
        q2 = jnp.concatenate([jnp.where(head0, q, zq), jnp.where(head0, zq, q)], axis=0)
        cq2 = jnp.concatenate([jnp.broadcast_to(
            jnp.sum(jnp.where(lane == pair * 2 + hh, cq_all, 0.0), axis=1, keepdims=True), (ts, ts))
            for hh in range(2)], axis=0)
        edge = lambda ks: (rel <= (qs0 - ks)) & ((ks + kcol) >= lead)

        def scores(ks, nb, bias, masked):
            zs = [_dot_nt(q2, k_ref[0, pl.ds(_aligned(ks + g * ts, ts), min(2, nb - g) * ts), :]) for g in range(0, nb, 2)]
            tiles = []
            for kk in range(nb):
                ck = jnp.concatenate(
                    [jnp.broadcast_to(ckt_ref[0, hh, :, pl.ds(_aligned(ks + kk * ts, ts), ts)], (ts, ts)) for hh in range(2)],
                    axis=0)
                t = zs[kk // 2][:, (kk % 2) * ts:(kk % 2 + 1) * ts] + bias - ck
                if masked:
                    vis = edge(ks + kk * ts)
                    t = jnp.where(jnp.concatenate([vis, vis], axis=0), t, NEG_INF)
                tiles.append(t)
            return tiles

        blk = lambda kb: pl.multiple_of(kb * ts, ts)
        n_mid = jnp.maximum(diag - 1, 0)
        n4 = lax.shift_right_logical(n_mid, 2)

        def sweep(f, carry):
            carry = f(0, carry, 1, True)
            carry = lax.fori_loop(0, n4, lambda j, c: f(blk(1 + 4 * j), c, 4, False), carry)
            carry = _loop_if((n_mid & 2) != 0, lambda c: f(blk(1 + 4 * n4), c, 2, False), carry)
            carry = _loop_if((n_mid & 1) != 0, lambda c: f(blk(diag - 1), c, 1, False), carry)
            return _loop_if(diag >= 1, lambda c: f(blk(diag), c, 1, True), carry)

        def row_max(ks, mx, nb, masked):
            for t in scores(ks, nb, cq2, masked):
                mx = jnp.maximum(mx, t)
            return mx

        mx = sweep(row_max, jnp.full((2 * ts, ts), NEG_INF, F32))
        m2 = jnp.broadcast_to(jnp.max(mx, axis=1, keepdims=True), (2 * ts, ts))
        cm2 = cq2 - m2

        def accumulate(ks, carry, nb, masked):
            acc, lsum = carry
            if masked:
                ps = [jnp.exp(t - m2) for t in scores(ks, nb, cq2, True)]
            else:
                ps = [jnp.exp(t) for t in scores(ks, nb, cm2, False)]
            for p in ps:
                lsum = lsum + p
            for g in range(0, nb, 2):
                grp = [p.astype(BF16) for p in ps[g:g + 2]]
                p2 = grp[0] if len(grp) == 1 else jnp.concatenate(grp, axis=1)
                acc = acc + _dot(p2, v_ref[0, pl.ds(_aligned(ks + g * ts, ts), len(grp) * ts), :])
            return acc, lsum

        zero = jnp.zeros((2 * ts, LANES), F32)
        acc, lsum = sweep(accumulate, (zero, zero))
        out = acc / jnp.sum(lsum, axis=1, keepdims=True)
        o_ref[0, pl.ds(r0, ts), :] = jnp.where(head0, out[:ts], out[ts:]).astype(o_ref.dtype)
        return 0

    lax.fori_loop(0, tq // ts, sub_tile, 0)


def fox_attention(q, k, v, cum, cum_t, *, tq, lead):
    b, lp, dq = q.shape
    return pl.pallas_call(
        functools.partial(_fox_attn_kernel, tq=tq, ts=ATTN_BLOCK, lead=lead),
        grid=(b, dq // LANES, lp // tq),
        in_specs=[
            pl.BlockSpec((1, tq, LANES), lambda i, p, j: (i, j, p)),
            pl.BlockSpec((1, lp, LANES), lambda i, p, j: (i, 0, p)),
            pl.BlockSpec((1, lp, LANES), lambda i, p, j: (i, 0, p)),
            pl.BlockSpec((1, tq, LANES), lambda i, p, j: (i, j, 0)),
            pl.BlockSpec((1, 2, 1, lp), lambda i, p, j: (i, p, 0, 0)),
        ],
        out_specs=pl.BlockSpec((1, tq, LANES), lambda i, p, j: (i, j, p)),
        out_shape=jax.ShapeDtypeStruct((b, lp, dq), BF16),
        compiler_params=_params(("parallel", "parallel", "arbitrary")),
        name="fox_attention",
    )(q, k, v, cum, cum_t)


def _logf_cum_kernel(fl_ref, bf_ref, logf_ref, cum_ref, cumt_ref, carry_ref, carryt_ref, *, lead):
    j = pl.program_id(1)

    @pl.when(j == 0)
    def _():
        carry_ref[...] = jnp.zeros_like(carry_ref)
        carryt_ref[...] = jnp.zeros_like(carryt_ref)

    x = fl_ref[0] + bf_ref[...]
    logf = jnp.minimum(x, 0.0) - jnp.log1p(jnp.exp(-jnp.abs(x)))
    logf_ref[0] = logf
    tb = logf.shape[0]
    pos = j * tb + lax.broadcasted_iota(jnp.int32, (tb, 1), 0)
    lz = jnp.where(pos >= lead, logf, 0.0)
    ri = lax.broadcasted_iota(jnp.int32, (tb, tb), 0)
    ci = lax.broadcasted_iota(jnp.int32, (tb, tb), 1)
    tril = (ci <= ri).astype(BF16)
    triu = (ri <= ci).astype(BF16)
    c = carry_ref[...]
    ct = carryt_ref[...]
    for p in _split3(lz):
        c = c + _dot(tril, p)
        ct = ct + _dot_tn(p, triu)
    cum_ref[0] = c
    cumt_ref[0] = ct
    carry_ref[...] = jnp.broadcast_to(c[tb - 1:tb, :], carry_ref.shape)
    carryt_ref[...] = jnp.broadcast_to(ct[:, tb - 1:tb], carryt_ref.shape)


def logf_cumsum(fl, b_f, *, lead):
    b, lp, _ = fl.shape
    tb = LANES
    return pl.pallas_call(
        functools.partial(_logf_cum_kernel, lead=lead),
        grid=(b, lp // tb),
        in_specs=[
            pl.BlockSpec((1, tb, LANES), lambda i, j: (i, j, 0)),
            pl.BlockSpec((1, LANES), lambda i, j: (0, 0)),
        ],
        out_specs=[
            pl.BlockSpec((1, tb, LANES), lambda i, j: (i, j, 0)),
            pl.BlockSpec((1, tb, LANES), lambda i, j: (i, j, 0)),
            pl.BlockSpec((1, LANES, tb), lambda i, j: (i, 0, j)),
        ],
        out_shape=[
            jax.ShapeDtypeStruct((b, lp, LANES), F32),
            jax.ShapeDtypeStruct((b, lp, LANES), F32),
            jax.ShapeDtypeStruct((b, LANES, lp), F32),
        ],
        scratch_shapes=[pltpu.VMEM((tb, LANES), F32), pltpu.VMEM((LANES, tb), F32)],
        compiler_params=_params(("parallel", "arbitrary")),
        name="logf_cumsum",
    )(fl, b_f)


def _pair_masks(rows):
    lane = lax.broadcasted_iota(jnp.int32, (rows, LANES), 1)
    return lane < HEAD_DIM, lane >= HEAD_DIM


def _bd(x):
    m0, m1 = _pair_masks(x.shape[0])
    zero = jnp.zeros_like(x)
    return jnp.concatenate([jnp.where(m0, x, zero), jnp.where(m1, x, zero)], axis=0)


def _rwkv_pair_chunk(r, k, v, av, b, ld, bds):
    c = r.shape[0]
    ri = lax.broadcasted_iota(jnp.int32, (c, c), 0)
    ci = lax.broadcasted_iota(jnp.int32, (c, c), 1)
    tril = (ci <= ri).astype(BF16)
    ones = jnp.ones((c, LANES), BF16)
    cs = jnp.zeros((c, LANES), F32)
    tot = jnp.zeros((LANES, LANES), F32)
    for part in _split3(ld):
        cs = cs + _dot(tril, part)
        tot = tot + _dot_tn(part, ones)
    e_pos = jnp.exp(cs)
    e_neg = jnp.exp(-cs)
    at = (av * jnp.exp(cs - ld)).astype(BF16)
    rt = (r * e_pos).astype(BF16)
    bt = b * e_neg
    kt = k * e_neg
    e_end = jnp.exp(cs[c - 1:c, :] - cs)
    bh = (b * e_end).astype(BF16)
    kh = (k * e_end).astype(BF16)

    lhs = jnp.concatenate([at, rt], axis=0)
    rhs = jnp.concatenate([_bd(bt), _bd(kt)], axis=0).astype(BF16)
    sc = _dot_nt(lhs, rhs)
    t2 = lax.broadcasted_iota(jnp.int32, (c, 2 * c), 0)
    s2 = lax.broadcasted_iota(jnp.int32, (c, 2 * c), 1) & (c - 1)
    strict = s2 < t2
    incl = s2 <= t2
    a_ab = jnp.where(strict, sc[:c, :2 * c], 0.0)
    a_ak = jnp.where(strict, sc[:c, 2 * c:], 0.0)
    a_rb = jnp.where(incl, sc[c:, :2 * c], 0.0)
    a_rk = jnp.where(incl, sc[c:, 2 * c:], 0.0)

    inv = jnp.where(s2 == t2, 1.0, 0.0) + a_ab
    lp = a_ab
    n_sq = max(c.bit_length() - 2, 0)
    for _ in range(n_sq):
        lp = _dot(lp.astype(BF16), _bd(lp).astype(BF16))
        inv = inv + _dot(inv.astype(BF16), _bd(lp).astype(BF16))

    bds16 = bds.astype(BF16)
    vbd = _bd(v).astype(BF16)
    rhs_u = _dot(at, bds16) + _dot(a_ak.astype(BF16), vbd)
    u = _dot(inv.astype(BF16), _bd(rhs_u).astype(BF16))
    ubd = _bd(u).astype(BF16)
    y = (_dot(rt, bds16)
         + _dot(jnp.concatenate([a_rb, a_rk], axis=1).astype(BF16), jnp.concatenate([ubd, vbd], axis=0)))
    cross = _dot_tn(bh, u.astype(BF16)) + _dot_tn(kh, v.astype(BF16))
    bds_new = jnp.exp(tot) * bds + jnp.where(_same_head(LANES, LANES), cross, 0.0)
    return y, bds_new


def _same_head(rows, cols):
    rr = lax.broadcasted_iota(jnp.int32, (rows, cols), 0) >= HEAD_DIM
    cc = lax.broadcasted_iota(jnp.int32, (rows, cols), 1) >= HEAD_DIM
    return rr == cc


def _head_sum(x, scale=1.0):
    g = jnp.where(_same_head(LANES, LANES), scale, 0.0).astype(BF16)
    hi, lo = _split2(x)
    return _dot(hi, g) + _dot(lo, g)


def _softplus(x):
    return jnp.maximum(x, 0.0) + jnp.log1p(jnp.exp(-jnp.abs(x)))


def _rwkv_prep(ps, w0, w2p, a0, a2p, g2):
    da = w0.shape[1]
    x12 = ps[:, 3 * da:3 * da + LANES]
    lg = ps[:, 3 * da + LANES:3 * da + 2 * LANES]
    w_log = -_softplus(-(w0 + _dot(jnp.tanh(x12).astype(BF16), w2p))) - 0.5
    ld = -jnp.exp(w_log)
    a = jax.nn.sigmoid(a0 + _dot(x12.astype(BF16), a2p))
    g = _dot(jax.nn.sigmoid(lg).astype(BF16), g2)
    return ld, a, g


def _rwkv_pair_inputs(ps, a, kk_w, ka_w, pr):
    da = a.shape[1]
    sl = slice(LANES * pr, LANES * (pr + 1))
    r = ps[:, sl]
    k = ps[:, da + LANES * pr:da + LANES * (pr + 1)]
    v = ps[:, 2 * da + LANES * pr:2 * da + LANES * (pr + 1)]
    kk = k * kk_w[:, sl]
    kk = kk / jnp.maximum(jnp.sqrt(_head_sum(kk * kk)), 1e-12)
    ap = a[:, sl]
    k2 = k * (1.0 + (ap - 1.0) * ka_w[:, sl])
    return r, k2, v, kk, ap


def _rwkv_pair_output(y, r, k2, v, g, rk_w, lnw, lnb, pr):
    sl = slice(LANES * pr, LANES * (pr + 1))
    mean = _head_sum(y, 1.0 / HEAD_DIM)
    d = y - mean
    var = _head_sum(d * d, 1.0 / HEAD_DIM)
    yn = d * lax.rsqrt(var + GN_EPS) * lnw[:, sl] + lnb[:, sl]
    bonus = _head_sum(r * k2 * rk_w[:, sl]) * v
    return (yn + bonus) * g[:, sl]


def _rwkv_mix_kernel(p_ref, mu_ref, w0_ref, w2_ref, a0_ref, a2_ref, g2_ref, kk_ref, ka_ref, rk_ref,
                     lnw_ref, lnb_ref, ya_ref, sfin_ref, prev_ref, st_ref, *, c, n_pairs):
    j = pl.program_id(1)

    @pl.when(j == 0)
    def _():
        prev_ref[...] = jnp.zeros_like(prev_ref)
        st_ref[...] = jnp.zeros_like(st_ref)

    p = p_ref[0]
    row = lax.broadcasted_iota(jnp.int32, (c, 1), 0)
    p_prev = jnp.where(row == 0, prev_ref[0:1, :], pltpu.roll(p, 1, 0))
    prev_ref[...] = jnp.broadcast_to(p[c - 1:c, :], prev_ref.shape)
    ps = p + (p_prev - p) * mu_ref[...]
    ld, a, g = _rwkv_prep(ps, w0_ref[...], w2_ref[...], a0_ref[...], a2_ref[...], g2_ref[...])
    for pr in range(n_pairs):
        sl = slice(LANES * pr, LANES * (pr + 1))
        r, k2, v, kk, ap = _rwkv_pair_inputs(ps, a, kk_ref[...], ka_ref[...], pr)
        y, bds = _rwkv_pair_chunk(r, k2, v, -kk, kk * ap, ld[:, sl], st_ref[pr])
        st_ref[pr] = bds
        out = _rwkv_pair_output(y, r, k2, v, g, rk_ref[...], lnw_ref[...], lnb_ref[...], pr)
        ya_ref[0, :, sl] = out.astype(ya_ref.dtype)

    @pl.when(j == pl.num_programs(1) - 1)
    def _():
        sfin_ref[0] = st_ref[...]


def rwkv_mix_prompt(p, mu, w0, w2p, a0, a2p, g2, kk_w, ka_w, rk_w, lnw, lnb):
    b, lp, ds = p.shape
    da = w0.shape[1]
    n_pairs = da // LANES
    c = RWKV_CHUNK
    vec = lambda n: pl.BlockSpec((1, n), lambda i, j: (0, 0))
    mat = lambda w: pl.BlockSpec(w.shape, lambda i, j: (0, 0))
    return pl.pallas_call(
        functools.partial(_rwkv_mix_kernel, c=c, n_pairs=n_pairs),
        grid=(b, lp // c),
        in_specs=[pl.BlockSpec((1, c, ds), lambda i, j: (i, j, 0)), vec(ds), vec(da), mat(w2p), vec(da), mat(a2p),
                  mat(g2), vec(da), vec(da), vec(da), vec(da), vec(da)],
        out_specs=[pl.BlockSpec((1, c, da), lambda i, j: (i, j, 0)),
                   pl.BlockSpec((1, n_pairs, LANES, LANES), lambda i, j: (i, 0, 0, 0))],
        out_shape=[jax.ShapeDtypeStruct((b, lp, da), BF16),
                   jax.ShapeDtypeStruct((b, n_pairs, LANES, LANES), F32)],
        scratch_shapes=[pltpu.VMEM((8, ds), F32), pltpu.VMEM((n_pairs, LANES, LANES), F32)],
        compiler_params=_params(("parallel", "arbitrary")),
        name="rwkv_mix_prompt",
    )(p, mu, w0, w2p, a0, a2p, g2, kk_w, ka_w, rk_w, lnw, lnb)


def _rwkv_step_kernel(p_ref, prev_ref, s_ref, mu_ref, w0_ref, w2_ref, a0_ref, a2_ref, g2_ref, kk_ref, ka_ref,
                      rk_ref, lnw_ref, lnb_ref, ya_ref, snew_ref, r_s, w_s, k_s, v_s, a_s, b_s, y_s,
                      *, n_pairs):
    p = p_ref[...]
    ps = p + (prev_ref[...] - p) * mu_ref[...]
    ld, a, g = _rwkv_prep(ps, w0_ref[...], w2_ref[...], a0_ref[...], a2_ref[...], g2_ref[...])
    pair_vals = []
    for pr in range(n_pairs):
        sl = slice(LANES * pr, LANES * (pr + 1))
        r, k2, v, kk, ap = _rwkv_pair_inputs(ps, a, kk_ref[...], ka_ref[...], pr)
        pair_vals.append((r, k2, v))
        r_s[:, sl] = r
        w_s[:, sl] = jnp.exp(ld[:, sl])
        k_s[:, sl] = k2
        v_s[:, sl] = v
        a_s[:, sl] = -kk
        b_s[:, sl] = kk * ap

    hd = HEAD_DIM
    eye = lax.broadcasted_iota(jnp.int32, (hd, hd), 0) == lax.broadcasted_iota(jnp.int32, (hd, hd), 1)
    sel = (lax.broadcasted_iota(jnp.int32, (hd, LANES), 1) & (hd - 1)) == lax.broadcasted_iota(
        jnp.int32, (hd, LANES), 0)
    lane = lax.broadcasted_iota(jnp.int32, (1, LANES), 1)

    def body(bg, carry):
        b0 = pl.multiple_of(bg * 8, 8)
        for pr in range(n_pairs):
            sl = slice(LANES * pr, LANES * (pr + 1))
            tiles = {}
            for name, ref in (("a", a_s), ("b", b_s), ("w", w_s), ("k", k_s), ("r", r_s), ("v", v_s)):
                x = ref[pl.ds(b0, 8), sl]
                tiles[name] = (x, pltpu.roll(x, hd, 1))
            y_rows = []
            for i in range(8):
                heads = []
                for hh in range(2):
                    vec = lambda name: tiles[name][hh][i:i + 1, :hd]
                    s = s_ref[b0 + i, 2 * pr + hh]
                    sa = jnp.sum(s * vec("a"), axis=1, keepdims=True)
                    v_col = jnp.sum(jnp.where(eye, vec("v"), 0.0), axis=1, keepdims=True)
                    s2 = s * vec("w") + sa * vec("b") + v_col * vec("k")
                    snew_ref[b0 + i, 2 * pr + hh] = s2
                    y_col = jnp.sum(s2 * vec("r"), axis=1, keepdims=True)
                    heads.append(jnp.sum(jnp.where(sel, y_col, 0.0), axis=0, keepdims=True))
                y_rows.append(jnp.where(lane < hd, heads[0], heads[1]))
            y_s[pl.ds(b0, 8), sl] = jnp.concatenate(y_rows, axis=0)
        return carry

    lax.fori_loop(0, p.shape[0] // 8, body, 0)
    y = y_s[...]
    for pr in range(n_pairs):
        sl = slice(LANES * pr, LANES * (pr + 1))
        r, k2, v = pair_vals[pr]
        out = _rwkv_pair_output(y[:, sl], r, k2, v, g, rk_ref[...], lnw_ref[...], lnb_ref[...], pr)
        ya_ref[:, sl] = out.astype(ya_ref.dtype)


def rwkv_step(p, prev, state, mu, w0, w2p, a0, a2p, g2, kk_w, ka_w, rk_w, lnw, lnb):
    db, ds = p.shape
    da = w0.shape[1]
    return pl.pallas_call(
        functools.partial(_rwkv_step_kernel, n_pairs=da // LANES),
        out_shape=[jax.ShapeDtypeStruct((db, da), BF16), jax.ShapeDtypeStruct(state.shape, F32)],
        scratch_shapes=[pltpu.VMEM((db, da), F32) for _ in range(7)],
        compiler_params=pltpu.CompilerParams(vmem_limit_bytes=VMEM_LIMIT),
        name="rwkv_step",
    )(p, prev, state, mu, w0, w2p, a0, a2p, g2, kk_w, ka_w, rk_w, lnw, lnb)


def _head_rows(q_row, rows):
    width = q_row.shape[1]
    row = lax.broadcasted_iota(jnp.int32, (rows, width), 0)
    col = lax.broadcasted_iota(jnp.int32, (rows, width), 1) >> 6
    return jnp.where(row == col, q_row, 0.0)


def _diag_blocks(acc):
    row = lax.broadcasted_iota(jnp.int32, acc.shape, 0)
    col = lax.broadcasted_iota(jnp.int32, acc.shape, 1) >> 6
    return jnp.sum(jnp.where(row == col, acc, 0.0), axis=0, keepdims=True)


def _log_sigmoid(x):
    return jnp.minimum(x, 0.0) - jnp.log1p(jnp.exp(-jnp.abs(x)))


def _sb_decode_kernel(pt_ref, q_ref, kn_ref, vn_ref, u2_ref, *rest, pp, n_pages, n_heads):
    k_refs, v_refs = rest[:pp], rest[pp:2 * pp]
    o_ref, qt_ref, acc_ref, run_ref = rest[2 * pp:]
    j = pl.program_id(1)
    qpos = n_pages * PAGE_SIZE
    lane = lax.broadcasted_iota(jnp.int32, (1, PAGE_SIZE), 1)

    def page(kpg, vpg, kpos0, n_valid):
        z = _dot_nt(qt_ref[...], kpg.astype(BF16))[:n_heads]
        vis = ((kpos0 + lane) < qpos) & (lane < n_valid)
        ls = _log_sigmoid(z)
        lr = jnp.where(vis, ls - z, 0.0)
        hi, lo = _split2(lr)
        s2 = _dot(jnp.concatenate([hi, lo], axis=1), u2_ref[...])
        a = jnp.where(vis, jnp.exp(ls + s2[:, :PAGE_SIZE] + run_ref[...]), 0.0)
        acc_ref[...] += _dot(a.astype(BF16), vpg.astype(BF16))
        run_ref[...] += s2[:, PAGE_SIZE:]

    @pl.when(j == 0)
    def _():
        qt_ref[...] = _head_rows(q_ref[0].astype(F32), qt_ref.shape[0]).astype(BF16)
        acc_ref[...] = jnp.zeros_like(acc_ref)
        run_ref[...] = jnp.zeros_like(run_ref)
        page(kn_ref[0], vn_ref[0], qpos, 1)

    for s in range(pp):
        page(k_refs[s][0], v_refs[s][0], (n_pages - 1 - (j * pp + s)) * PAGE_SIZE, PAGE_SIZE)

    @pl.when(j == pl.num_programs(1) - 1)
    def _():
        o_ref[0] = _diag_blocks(acc_ref[...]).astype(o_ref.dtype)


def _page_specs(pp, n_pages, width):
    def spec(s):
        return pl.BlockSpec((1, PAGE_SIZE, width), lambda b, j, pt: (pt[b, n_pages - 1 - (j * pp + s)], 0, 0))
    return [spec(s) for s in range(pp)]


def sb_decode(q, k_new, v_new, cache_k, cache_v, page_table, *, pp):
    db, _, d = q.shape
    n_pages = page_table.shape[1]
    n_heads = d // HEAD_DIM
    u2 = _suffix_matrix(PAGE_SIZE)
    row = lambda r, w: pl.BlockSpec((1, r, w), lambda b, j, pt: (b, 0, 0))
    grid_spec = pltpu.PrefetchScalarGridSpec(
        num_scalar_prefetch=1,
        grid=(db, n_pages // pp),
        in_specs=[row(1, d), row(PAGE_SIZE, d), row(PAGE_SIZE, d),
                  pl.BlockSpec(u2.shape, lambda b, j, pt: (0, 0))]
                 + _page_specs(pp, n_pages, d) + _page_specs(pp, n_pages, d),
        out_specs=row(1, d),
        scratch_shapes=[pltpu.VMEM((16, d), BF16), pltpu.VMEM((n_heads, d), F32),
                        pltpu.VMEM((n_heads, PAGE_SIZE), F32)],
    )
    return pl.pallas_call(
        functools.partial(_sb_decode_kernel, pp=pp, n_pages=n_pages, n_heads=n_heads),
        grid_spec=grid_spec,
        out_shape=jax.ShapeDtypeStruct((db, 1, d), BF16),
        compiler_params=_params(("parallel", "arbitrary")),
        name="sb_decode",
    )(page_table, q, k_new, v_new, u2, *([cache_k] * pp), *([cache_v] * pp))


def _fox_decode_kernel(pt_ref, q_ref, kn_ref, vn_ref, fl_ref, bf_ref, m2_ref, *rest, pp, n_pages, n_heads):
    k_refs, v_refs, lf_refs = rest[:pp], rest[pp:2 * pp], rest[2 * pp:3 * pp]
    o_ref, lfo_ref, qt_ref, acc_ref, m_ref, l_ref, run_ref = rest[3 * pp:]
    j = pl.program_id(1)
    qpos = n_pages * PAGE_SIZE
    lane = lax.broadcasted_iota(jnp.int32, (1, PAGE_SIZE), 1)

    def page(kpg, vpg, bias, kpos0, n_valid):
        z = _dot_nt(qt_ref[...], kpg.astype(BF16))[:n_heads] + bias
        vis = ((kpos0 + lane) <= qpos) & (lane < n_valid)
        z = jnp.where(vis, z, NEG_INF)
        m_new = jnp.maximum(m_ref[...], jnp.max(z, axis=1, keepdims=True))
        alpha = jnp.exp(m_ref[...] - m_new)
        p = jnp.exp(z - m_new)
        l_ref[...] = alpha * l_ref[...] + jnp.sum(p, axis=1, keepdims=True)
        acc_ref[...] = alpha * acc_ref[...] + _dot(p.astype(BF16), vpg.astype(BF16))
        m_ref[...] = m_new

    @pl.when(j == 0)
    def _():
        qt_ref[...] = _head_rows(q_ref[0].astype(F32), qt_ref.shape[0]).astype(BF16)
        acc_ref[...] = jnp.zeros_like(acc_ref)
        m_ref[...] = jnp.full_like(m_ref, NEG_INF)
        l_ref[...] = jnp.zeros_like(l_ref)
        logf = _log_sigmoid(fl_ref[0] + bf_ref[...])
        lfo_ref[0] = logf
        eye = lax.broadcasted_iota(jnp.int32, (LANES, LANES), 0) == lax.broadcasted_iota(
            jnp.int32, (LANES, LANES), 1)
        col = jnp.sum(jnp.where(eye, logf, 0.0), axis=1, keepdims=True)
        run_ref[...] = jnp.broadcast_to(col[:n_heads], run_ref.shape)
        page(kn_ref[0], vn_ref[0], 0.0, qpos, 1)

    for s in range(pp):
        lf = lf_refs[s][0]
        s2 = jnp.zeros((n_heads, 2 * PAGE_SIZE), F32)
        for part in _split3(lf):
            s2 = s2 + _dot(part, m2_ref[...])
        page(k_refs[s][0], v_refs[s][0], run_ref[...] + s2[:, :PAGE_SIZE],
             (n_pages - 1 - (j * pp + s)) * PAGE_SIZE, PAGE_SIZE)
        run_ref[...] += s2[:, PAGE_SIZE:]

    @pl.when(j == pl.num_programs(1) - 1)
    def _():
        o_ref[0] = _diag_blocks(acc_ref[...] / l_ref[...]).astype(o_ref.dtype)


def fox_decode(q, k_new, v_new, fl_new, b_f, cache_k, cache_v, cache_lf_t, page_table, *, pp):
    db, _, d = q.shape
    n_pages = page_table.shape[1]
    n_heads = d // HEAD_DIM
    jj = jnp.arange(PAGE_SIZE)[:, None]
    ss = jnp.arange(PAGE_SIZE)[None, :]
    m2 = jnp.concatenate([(jj > ss).astype(BF16), jnp.ones((PAGE_SIZE, PAGE_SIZE), BF16)], axis=1)
    row = lambda r, w: pl.BlockSpec((1, r, w), lambda b, j, pt: (b, 0, 0))
    lf_specs = [pl.BlockSpec((1, n_heads, PAGE_SIZE),
                             lambda b, j, pt, s=s: (pt[b, n_pages - 1 - (j * pp + s)], 0, 0)) for s in range(pp)]
    grid_spec = pltpu.PrefetchScalarGridSpec(
        num_scalar_prefetch=1,
        grid=(db, n_pages // pp),
        in_specs=[row(1, d), row(PAGE_SIZE, d), row(PAGE_SIZE, d), row(1, LANES),
                  pl.BlockSpec((1, LANES), lambda b, j, pt: (0, 0)),
                  pl.BlockSpec(m2.shape, lambda b, j, pt: (0, 0))]
                 + _page_specs(pp, n_pages, d) + _page_specs(pp, n_pages, d) + lf_specs,
        out_specs=[row(1, d), row(1, LANES)],
        scratch_shapes=[pltpu.VMEM((16, d), BF16), pltpu.VMEM((n_heads, d), F32),
                        pltpu.VMEM((n_heads, 1), F32), pltpu.VMEM((n_heads, 1), F32),
                        pltpu.VMEM((n_heads, PAGE_SIZE), F32)],
    )
    return pl.pallas_call(
        functools.partial(_fox_decode_kernel, pp=pp, n_pages=n_pages, n_heads=n_heads),
        grid_spec=grid_spec,
        out_shape=[jax.ShapeDtypeStruct((db, 1, d), BF16), jax.ShapeDtypeStruct((db, 1, LANES), F32)],
        compiler_params=_params(("parallel", "arbitrary")),
        name="fox_decode",
    )(page_table, q, k_new, v_new, fl_new, b_f, m2, *([cache_k] * pp), *([cache_v] * pp), *([cache_lf_t] * pp))


def _strided_suffix(x, stride):
    n = x.shape[1]
    lane = lax.broadcasted_iota(jnp.int32, (1, n), 1)
    sh = stride
    while sh < n:
        x = x + jnp.where(lane < n - sh, pltpu.roll(x, n - sh, 1), 0.0)
        sh *= 2
    return x


def _lane_tile(x, stride):
    n = x.shape[1]
    lane = lax.broadcasted_iota(jnp.int32, (1, n), 1)
    t = jnp.where(lane < stride, x, 0.0)
    sh = stride
    while sh < n:
        t = t + pltpu.roll(t, sh, 1)
        sh *= 2
    return t


def _head_match(n_heads, n):
    lane = lax.broadcasted_iota(jnp.int32, (n_heads, n), 1)
    row = lax.broadcasted_iota(jnp.int32, (n_heads, n), 0)
    shift = n_heads.bit_length() - 1
    return (lane & (n_heads - 1)) == row, lane >> shift


def _query_rows(q_ref):
    q = q_ref[0]
    if q.shape[0] < 16:
        q = jnp.concatenate([q, jnp.zeros((16 - q.shape[0], q.shape[1]), F32)], axis=0)
    return q.astype(BF16)


def _sb_decode_kernel2(pt_ref, q_ref, kn_ref, vn_ref, *rest, pp, n_pages, n_heads):
    k_refs, v_refs = rest[:pp], rest[pp:2 * pp]
    o_ref, acc_ref, run_ref = rest[2 * pp:]
    j = pl.program_id(1)
    qpos = n_pages * PAGE_SIZE
    qh = _query_rows(q_ref)

    def page(state, kpg, vpg, kpos0, n_valid):
        acc, run = state
        n = kpg.shape[0] * n_heads
        kf = kpg.reshape(n, HEAD_DIM).astype(BF16)
        vf = vpg.reshape(n, HEAD_DIM).astype(BF16)
        match, srow = _head_match(n_heads, n)
        z = _dot_nt(qh, kf)[:n_heads]
        vis = match & ((kpos0 + srow) < qpos) & (srow < n_valid)
        ls = _log_sigmoid(z)
        lr = jnp.where(vis, ls - z, 0.0)
        incl = _strided_suffix(lr, n_heads)
        a = jnp.where(vis, jnp.exp(ls + (incl - lr) + run[:, :n]), 0.0)
        tot = jnp.sum(jnp.where(match & (srow == 0), incl, 0.0), axis=0, keepdims=True)
        if n < run.shape[1]:
            tot = jnp.concatenate([tot, jnp.zeros((1, run.shape[1] - n), F32)], axis=1)
        return acc + _dot(a.astype(BF16), vf), run + _lane_tile(tot, n_heads)

    @pl.when(j == 0)
    def _():
        zero = (jnp.zeros(acc_ref.shape, F32), jnp.zeros(run_ref.shape, F32))
        acc_ref[...], run_ref[...] = page(zero, kn_ref[0], vn_ref[0], qpos, 1)

    state = (acc_ref[...], run_ref[...])
    for s in range(pp):
        state = page(state, k_refs[s][0, 0], v_refs[s][0, 0], (n_pages - 1 - (j * pp + s)) * PAGE_SIZE, PAGE_SIZE)
    acc_ref[...], run_ref[...] = state

    @pl.when(j == pl.num_programs(1) - 1)
    def _():
        o_ref[0] = state[0].astype(o_ref.dtype)


def _cache_page_specs(layer, pp, n_pages, n_heads):
    return [pl.BlockSpec((1, 1, PAGE_SIZE, n_heads, HEAD_DIM),
                         lambda b, j, pt, s=s: (layer, pt[b, n_pages - 1 - (j * pp + s)], 0, 0, 0))
            for s in range(pp)]


def sb_decode_paged(q, k_new, v_new, cache_k, cache_v, layer, page_table, *, pp):
    db, n_heads, _ = q.shape
    n_pages = page_table.shape[1]
    new_rows = k_new.shape[1]
    per_b = lambda shape: pl.BlockSpec((1,) + shape, lambda b, j, pt: (b,) + (0,) * len(shape))
    grid_spec = pltpu.PrefetchScalarGridSpec(
        num_scalar_prefetch=1,
        grid=(db, n_pages // pp),
        in_specs=[per_b((n_heads, HEAD_DIM)), per_b((new_rows, n_heads, HEAD_DIM)),
                  per_b((new_rows, n_heads, HEAD_DIM))]
                 + _cache_page_specs(layer, pp, n_pages, n_heads) + _cache_page_specs(layer, pp, n_pages, n_heads),
        out_specs=per_b((n_heads, HEAD_DIM)),
        scratch_shapes=[pltpu.VMEM((n_heads, HEAD_DIM), F32), pltpu.VMEM((1, PAGE_SIZE * n_heads), F32)],
    )
    return pl.pallas_call(
        functools.partial(_sb_decode_kernel2, pp=pp, n_pages=n_pages, n_heads=n_heads),
        grid_spec=grid_spec,
        out_shape=jax.ShapeDtypeStruct((db, n_heads, HEAD_DIM), BF16),
        compiler_params=_params(("parallel", "arbitrary")),
        name="sb_decode",
    )(page_table, q, k_new, v_new, *([cache_k] * pp), *([cache_v] * pp))


def _fox_decode_kernel2(pt_ref, q_ref, kn_ref, vn_ref, fl_ref, bf_ref, *rest, pp, n_pages, n_heads):
    k_refs, v_refs, lf_refs = rest[:pp], rest[pp:2 * pp], rest[2 * pp:3 * pp]
    o_ref, lfo_ref, acc_ref, m_ref, l_ref, run_ref = rest[3 * pp:]
    j = pl.program_id(1)
    qpos = n_pages * PAGE_SIZE
    qh = _query_rows(q_ref)

    def page(state, kpg, vpg, bias, kpos0, n_valid):
        acc, m, l = state
        n = kpg.shape[0] * n_heads
        kf = kpg.reshape(n, HEAD_DIM).astype(BF16)
        vf = vpg.reshape(n, HEAD_DIM).astype(BF16)
        match, srow = _head_match(n_heads, n)
        z = _dot_nt(qh, kf)[:n_heads] + bias
        vis = match & ((kpos0 + srow) <= qpos) & (srow < n_valid)
        z = jnp.where(vis, z, NEG_INF)
        m_new = jnp.maximum(m, jnp.max(z, axis=1, keepdims=True))
        alpha = jnp.exp(m - m_new)
        p = jnp.where(vis, jnp.exp(z - m_new), 0.0)
        return (alpha * acc + _dot(p.astype(BF16), vf), m_new,
                alpha * l + jnp.sum(p, axis=1, keepdims=True))

    @pl.when(j == 0)
    def _():
        logf = _log_sigmoid(fl_ref[0] + bf_ref[...])
        lfo_ref[0] = logf
        wide = jnp.concatenate([logf, jnp.zeros((1, run_ref.shape[1] - LANES), F32)], axis=1)
        run_ref[...] = _lane_tile(wide, n_heads)
        init = (jnp.zeros(acc_ref.shape, F32), jnp.full(m_ref.shape, NEG_INF, F32), jnp.zeros(l_ref.shape, F32))
        acc_ref[...], m_ref[...], l_ref[...] = page(init, kn_ref[0], vn_ref[0], 0.0, qpos, 1)

    state = (acc_ref[...], m_ref[...], l_ref[...])
    run = run_ref[...]
    for s in range(pp):
        lf = lf_refs[s][0]
        incl = _strided_suffix(lf, n_heads)
        state = page(state, k_refs[s][0, 0], v_refs[s][0, 0], run + (incl - lf),
                     (n_pages - 1 - (j * pp + s)) * PAGE_SIZE, PAGE_SIZE)
        run = run + _lane_tile(incl, n_heads)
    acc_ref[...], m_ref[...], l_ref[...] = state
    run_ref[...] = run

    @pl.when(j == pl.num_programs(1) - 1)
    def _():
        o_ref[0] = (state[0] / state[2]).astype(o_ref.dtype)


def fox_decode_paged(q, k_new, v_new, fl_new, b_f, cache_k, cache_v, layer, cache_lf_rows, page_table, *, pp):
    db, n_heads, _ = q.shape
    n_pages = page_table.shape[1]
    new_rows = k_new.shape[1]
    n = PAGE_SIZE * n_heads
    per_b = lambda shape: pl.BlockSpec((1,) + shape, lambda b, j, pt: (b,) + (0,) * len(shape))
    lf_specs = [pl.BlockSpec((1, 1, n), lambda b, j, pt, s=s: (pt[b, n_pages - 1 - (j * pp + s)], 0, 0))
                for s in range(pp)]
    grid_spec = pltpu.PrefetchScalarGridSpec(
        num_scalar_prefetch=1,
        grid=(db, n_pages // pp),
        in_specs=[per_b((n_heads, HEAD_DIM)), per_b((new_rows, n_heads, HEAD_DIM)),
                  per_b((new_rows, n_heads, HEAD_DIM)), per_b((1, LANES)),
                  pl.BlockSpec((1, LANES), lambda b, j, pt: (0, 0))]
                 + _cache_page_specs(layer, pp, n_pages, n_heads) + _cache_page_specs(layer, pp, n_pages, n_heads)
                 + lf_specs,
        out_specs=[per_b((n_heads, HEAD_DIM)), per_b((1, LANES))],
        scratch_shapes=[pltpu.VMEM((n_heads, HEAD_DIM), F32), pltpu.VMEM((n_heads, 1), F32),
                        pltpu.VMEM((n_heads, 1), F32), pltpu.VMEM((1, n), F32)],
    )
    return pl.pallas_call(
        functools.partial(_fox_decode_kernel2, pp=pp, n_pages=n_pages, n_heads=n_heads),
        grid_spec=grid_spec,
        out_shape=[jax.ShapeDtypeStruct((db, n_heads, HEAD_DIM), BF16), jax.ShapeDtypeStruct((db, 1, LANES), F32)],
        compiler_params=_params(("parallel", "arbitrary")),
        name="fox_decode",
    )(page_table, q, k_new, v_new, fl_new, b_f, *([cache_k] * pp), *([cache_v] * pp), *([cache_lf_rows] * pp))


def _new_token_rows(x, n_heads):
    db = x.shape[0]
    x4 = x.reshape(db, 1, n_heads, HEAD_DIM)
    return jnp.pad(x4, ((0, 0), (0, LANES // n_heads - 1), (0, 0), (0, 0)))


def _pad_lanes(x):
    return jnp.pad(x, [(0, 0)] * (x.ndim - 1) + [(0, LANES - x.shape[-1])])


def kernel(x_prompt, x_sample, cache_sb_k, cache_sb_v, cache_fox_k, cache_fox_v, cache_fox_logf, state_rwkv, state_rwkv_shift, page_table, meta_tokens, norm_g, ffn_w1, ffn_w3, ffn_w2, w_in_even, w_out_even, rwkv_mu, rwkv_w0, rwkv_w2, rwkv_a0, rwkv_a2, rwkv_g2, rwkv_kk, rwkv_ka, rwkv_rk, rwkv_lnw, rwkv_lnb, w_in_odd, b_f, w_out_odd):
    bsz, seq, d = x_prompt.shape
    db, ts, _ = x_sample.shape
    assert ts == 1, "the decode kernels handle one new token per sequence"
    depth = norm_g.shape[0]
    t_p = N_META + seq
    lp = LEAD + t_p
    n_pool, n_pages = cache_sb_k.shape[1], page_table.shape[1]
    h_a = state_rwkv.shape[2]
    d_a = h_a * HEAD_DIM
    d_b = cache_sb_k.shape[3] * HEAD_DIM
    d_c = cache_fox_k.shape[3] * HEAD_DIM
    h_c = d_c // HEAD_DIM
    d_shift = state_rwkv_shift.shape[2]
    dff = ffn_w1.shape[3]

    hp = jnp.concatenate([jnp.zeros((bsz, LEAD, d), F32),
                          jnp.broadcast_to(meta_tokens[None].astype(F32), (bsz, N_META, d)), x_prompt], axis=1)
    hs = x_sample.reshape(db, d)

    tm = _row_tile(lp)
    tq = tm
    tf = dff // 2 if (dff // 2) % LANES == 0 else dff
    pp = min(8, n_pages)
    gvec = lambda l, i: norm_g[l, i][None, :]

    def ffn(h2, l, i):
        w1, w3, w2 = (w[l, i].astype(BF16) for w in (ffn_w1, ffn_w3, ffn_w2))
        return ffn_half(h2, gvec(l, 2 * i * 2), gvec(l, 2 * i * 2 + 1), w1, w3, w2,
                        tm=_row_tile(h2.shape[0]), tf=tf)

    outs_p = {k: [] for k in ("sb_k", "sb_v", "fk", "fv", "fl", "rw", "sh")}
    outs_s = {k: [] for k in ("sb_k", "sb_v", "fk", "fv", "fl", "rw", "sh")}

    for l in range(depth):
        hp = ffn(hp.reshape(bsz * lp, d), l, 0).reshape(bsz, lp, d)
        hs = ffn(hs, l, 0)
        g_mix, g_post = gvec(l, 2), gvec(l, 3)
        if l % 2 == 0:
            e = l // 2
            w_in = w_in_even[e].astype(BF16)
            slabs = [w_in[:, :d_shift], w_in[:, d_shift:d_shift + d_b],
                     w_in[:, d_shift + d_b:d_shift + 2 * d_b], w_in[:, d_shift + 2 * d_b:]]
            dts = [(F32,), (BF16,), (F32, BF16), (F32, BF16)]
            scl = [1.0, HEAD_DIM ** -0.5, 1.0, 1.0]
            zeros_l = jnp.zeros((LORA_W, d_a), F32)
            rw = (rwkv_mu[e][None], rwkv_w0[e][None],
                  jnp.concatenate([rwkv_w2[e], zeros_l], axis=0).astype(BF16), rwkv_a0[e][None],
                  jnp.concatenate([zeros_l, rwkv_a2[e]], axis=0).astype(BF16), rwkv_g2[e].astype(BF16),
                  rwkv_kk[e][None], rwkv_ka[e][None], rwkv_rk[e].reshape(1, d_a), rwkv_lnw[e][None],
                  rwkv_lnb[e][None])
            w_out = w_out_even[e].astype(BF16)
            p, q, k, kb, v, vb = norm_proj(hp.reshape(bsz * lp, d), g_mix, slabs, dts, scl, tm=tm)
            r3 = lambda x: x.reshape(bsz, lp, x.shape[-1])
            ya, sfin = rwkv_mix_prompt(r3(p), *rw)
            yb = sb_attention(r3(q), r3(kb), r3(vb), tq=tq, lead=LEAD)
            hp = out_proj(hp, g_post, [ya, yb], [w_out[:d_a], w_out[d_a:]], tm=tm, lead=LEAD)
            outs_p["sb_k"].append(r3(k)[:, LEAD:].reshape(bsz, t_p, -1, HEAD_DIM))
            outs_p["sb_v"].append(r3(v)[:, LEAD:].reshape(bsz, t_p, -1, HEAD_DIM))
            s_pairs = sfin.reshape(bsz, d_a // LANES, 2, HEAD_DIM, 2, HEAD_DIM)
            s_heads = jnp.stack([s_pairs[:, :, 0, :, 0, :], s_pairs[:, :, 1, :, 1, :]], axis=2)
            outs_p["rw"].append(jnp.swapaxes(s_heads.reshape(bsz, h_a, HEAD_DIM, HEAD_DIM), -1, -2))
            outs_p["sh"].append(r3(p)[:, -1])
            p, q, k, v = norm_proj(hs, g_mix, slabs, [(F32,)] * 4, scl, tm=_row_tile(db))
            ya, s_new = rwkv_step(p, state_rwkv_shift[e], state_rwkv[e], *rw)
            h_b = d_b // HEAD_DIM
            yb = sb_decode_paged(q.reshape(db, h_b, HEAD_DIM), _new_token_rows(k, h_b), _new_token_rows(v, h_b),
                                 cache_sb_k, cache_sb_v, e, page_table, pp=min(8, n_pages))
            hs = out_proj(hs[None], g_post, [ya[None], yb.reshape(1, db, d_b)], [w_out[:d_a], w_out[d_a:]],
                          tm=_row_tile(db), lead=0)[0]
            outs_s["sb_k"].append(k.reshape(db, 1, -1, HEAD_DIM))
            outs_s["sb_v"].append(v.reshape(db, 1, -1, HEAD_DIM))
            outs_s["rw"].append(s_new)
            outs_s["sh"].append(p)
        else:
            o = l // 2
            w_in = w_in_odd[o]
            slabs = [w_in[:, :d_c].astype(BF16), w_in[:, d_c:2 * d_c].astype(BF16),
                     w_in[:, 2 * d_c:3 * d_c].astype(BF16), _pad_lanes(w_in[:, 3 * d_c:]).astype(BF16)]
            dts = [(BF16,), (F32, BF16), (F32, BF16), (F32,)]
            scl = [HEAD_DIM ** -0.5, 1.0, 1.0, 1.0]
            bf_row = _pad_lanes(b_f[o][None])
            w_out = w_out_odd[o].astype(BF16)
            q, k, kb, v, vb, fl = norm_proj(hp.reshape(bsz * lp, d), g_mix, slabs, dts, scl, tm=tm)
            r3 = lambda x: x.reshape(bsz, lp, x.shape[-1])
            logf, cum, cum_t = logf_cumsum(r3(fl), bf_row, lead=LEAD)
            yc = fox_attention(r3(q), r3(kb), r3(vb), cum, cum_t[:, :h_c, None, :], tq=tq, lead=LEAD)
            hp = out_proj(hp, g_post, [yc], [w_out], tm=tm, lead=LEAD)
            outs_p["fk"].append(r3(k)[:, LEAD:].reshape(bsz, t_p, h_c, HEAD_DIM))
            outs_p["fv"].append(r3(v)[:, LEAD:].reshape(bsz, t_p, h_c, HEAD_DIM))
            outs_p["fl"].append(logf[:, LEAD:, :h_c])
            q, k, v, fl = norm_proj(hs, g_mix, slabs, [(F32,)] * 4, scl, tm=_row_tile(db))
            yc, logf_s = fox_decode_paged(q.reshape(db, h_c, HEAD_DIM), _new_token_rows(k, h_c),
                                          _new_token_rows(v, h_c), fl[:, None], bf_row, cache_fox_k, cache_fox_v, o,
                                          cache_fox_logf[o].reshape(n_pool, 1, PAGE_SIZE * h_c), page_table,
                                          pp=min(4, n_pages))
            hs = out_proj(hs[None], g_post, [yc.reshape(1, db, d_c)], [w_out], tm=_row_tile(db), lead=0)[0]
            outs_s["fk"].append(k.reshape(db, 1, h_c, HEAD_DIM))
            outs_s["fv"].append(v.reshape(db, 1, h_c, HEAD_DIM))
            outs_s["fl"].append(logf_s[:, :, :h_c])
        hp = ffn(hp.reshape(bsz * lp, d), l, 1).reshape(bsz, lp, d)
        hs = ffn(hs, l, 1)

    y_prompt = hp[:, LEAD + N_META:]
    y_sample = hs.reshape(db, 1, d)
    order = ("sb_k", "sb_v", "fk", "fv", "fl", "rw", "sh")
    return (y_prompt, y_sample) + tuple(jnp.stack(outs_p[k]) for k in order) + tuple(
        jnp.stack(outs_s[k]) for k in order)
```

```python
import functools

import jax
import jax.numpy as jnp
from jax import lax
from jax.experimental import pallas as pl
from jax.experimental.pallas import tpu as pltpu

F32 = jnp.float32
BF16 = jnp.bfloat16

HEAD_DIM = 64
N_META = 16
ATTN_BLOCK = 128
LEAD = (-N_META) % ATTN_BLOCK
PAGE_SIZE = 128
LORA_W = 64
LORA_A = 64
LORA_G = 128
FFN_RES = 0.5
NORM_EPS = 1e-6
GN_EPS = 64e-5
NEG_INF = -1e30
LANES = 128
RWKV_CHUNK = 64
VMEM_LIMIT = 52 * 1024 * 1024


def _params(sem):
    return pltpu.CompilerParams(dimension_semantics=sem, vmem_limit_bytes=VMEM_LIMIT)


def _row_tile(m, candidates=(384, 256, 128, 64, 32, 16, 8)):
    for t in candidates:
        if m % t == 0:
            return t
    raise ValueError(f"no row tile for {m}")


def _rms(x, g):
    return x * lax.rsqrt(jnp.mean(x * x, axis=-1, keepdims=True) + NORM_EPS) * g


def _split3(x):
    h = x.astype(BF16)
    r = x - h.astype(F32)
    m = r.astype(BF16)
    l = (r - m.astype(F32)).astype(BF16)
    return h, m, l


def _split2(x):
    h = x.astype(BF16)
    return h, (x - h.astype(F32)).astype(BF16)


def _dot(a, b):
    return jnp.dot(a, b, preferred_element_type=F32)


def _dot_nt(a, b):
    return lax.dot_general(a, b, (((1,), (1,)), ((), ())), preferred_element_type=F32)


def _dot_tn(a, b):
    return lax.dot_general(a, b, (((0,), (0,)), ((), ())), preferred_element_type=F32)


def _ffn_kernel(h_ref, gpre_ref, gpost_ref, w1_ref, w3_ref, w2_ref, o_ref, xn_ref, acc_ref):
    j = pl.program_id(1)

    @pl.when(j == 0)
    def _():
        xn_ref[...] = _rms(h_ref[...], gpre_ref[...]).astype(BF16)
        acc_ref[...] = jnp.zeros_like(acc_ref)

    x = xn_ref[...]
    a = _dot(x, w1_ref[...])
    b = _dot(x, w3_ref[...])
    t = (a * jax.nn.sigmoid(a) * b).astype(BF16)
    acc_ref[...] += _dot(t, w2_ref[...])

    @pl.when(j == pl.num_programs(1) - 1)
    def _():
        o_ref[...] = h_ref[...] + FFN_RES * _rms(acc_ref[...], gpost_ref[...])


def ffn_half(h, g_pre, g_post, w1, w3, w2, *, tm, tf):
    m, d = h.shape
    dff = w1.shape[1]
    return pl.pallas_call(
        _ffn_kernel,
        grid=(m // tm, dff // tf),
        in_specs=[
            pl.BlockSpec((tm, d), lambda i, j: (i, 0)),
            pl.BlockSpec((1, d), lambda i, j: (0, 0)),
            pl.BlockSpec((1, d), lambda i, j: (0, 0)),
            pl.BlockSpec((d, tf), lambda i, j: (0, j)),
            pl.BlockSpec((d, tf), lambda i, j: (0, j)),
            pl.BlockSpec((tf, d), lambda i, j: (j, 0)),
        ],
        out_specs=pl.BlockSpec((tm, d), lambda i, j: (i, 0)),
        out_shape=jax.ShapeDtypeStruct((m, d), F32),
        scratch_shapes=[pltpu.VMEM((tm, d), BF16), pltpu.VMEM((tm, d), F32)],
        compiler_params=_params(("parallel", "arbitrary")),
        name="ffn_half",
    )(h, g_pre, g_post, w1, w3, w2)


def _proj_kernel(*refs, out_dtypes, scales):
    h_ref, g_ref = refs[0], refs[1]
    n_w = len(out_dtypes)
    w_refs = refs[2:2 + n_w]
    o_refs = iter(refs[2 + n_w:])
    x = _rms(h_ref[...], g_ref[...]).astype(BF16)
    for w_ref, dts, s in zip(w_refs, out_dtypes, scales):
        y = _dot(x, w_ref[...])
        if s != 1.0:
            y = y * s
        for dt in dts:
            next(o_refs)[...] = y.astype(dt)


def norm_proj(h, g, weights, out_dtypes, scales, *, tm):
    m, d = h.shape
    in_specs = [pl.BlockSpec((tm, d), lambda i: (i, 0)), pl.BlockSpec((1, d), lambda i: (0, 0))]
    in_specs += [pl.BlockSpec(w.shape, lambda i: (0, 0)) for w in weights]
    outs = [(w.shape[1], dt) for w, dts in zip(weights, out_dtypes) for dt in dts]
    return pl.pallas_call(
        functools.partial(_proj_kernel, out_dtypes=tuple(tuple(d) for d in out_dtypes), scales=tuple(scales)),
        grid=(m // tm,),
        in_specs=in_specs,
        out_specs=[pl.BlockSpec((tm, n), lambda i: (i, 0)) for n, _ in outs],
        out_shape=[jax.ShapeDtypeStruct((m, n), dt) for n, dt in outs],
        compiler_params=_params(("parallel",)),
        name="norm_proj",
    )(h, g, *weights)


def _outproj_kernel(*refs, n_in, tm, lead):
    h_ref, g_ref = refs[0], refs[1]
    y_refs = refs[2:2 + n_in]
    w_refs = refs[2 + n_in:2 + 2 * n_in]
    o_ref = refs[2 + 2 * n_in]
    m = _dot(y_refs[0][0], w_refs[0][...])
    for y_ref, w_ref in zip(y_refs[1:], w_refs[1:]):
        m = m + _dot(y_ref[0], w_ref[...])
    out = h_ref[0] + _rms(m, g_ref[...])
    pos = pl.program_id(1) * tm + lax.broadcasted_iota(jnp.int32, (tm, 1), 0)
    o_ref[0] = jnp.where(pos >= lead, out, 0.0)


def out_proj(h, g, ys, ws, *, tm, lead):
    b, lp, d = h.shape
    n = len(ys)
    in_specs = [pl.BlockSpec((1, tm, d), lambda i, j: (i, j, 0)), pl.BlockSpec((1, d), lambda i, j: (0, 0))]
    in_specs += [pl.BlockSpec((1, tm, y.shape[2]), lambda i, j: (i, j, 0)) for y in ys]
    in_specs += [pl.BlockSpec(w.shape, lambda i, j: (0, 0)) for w in ws]
    return pl.pallas_call(
        functools.partial(_outproj_kernel, n_in=n, tm=tm, lead=lead),
        grid=(b, lp // tm),
        in_specs=in_specs,
        out_specs=pl.BlockSpec((1, tm, d), lambda i, j: (i, j, 0)),
        out_shape=jax.ShapeDtypeStruct((b, lp, d), F32),
        compiler_params=_params(("parallel", "parallel")),
        name="out_proj",
    )(h, g, *ys, *ws)


def _sb_attn_kernel(q_ref, k_ref, v_ref, u2_ref, o_ref, *, tq, ts, lead):
    q0 = pl.program_id(2) * tq
    lane = lax.broadcasted_iota(jnp.int32, (1, LANES), 1)
    head0 = lane < HEAD_DIM
    rel = lax.broadcasted_iota(jnp.int32, (ts, ts), 1) - lax.broadcasted_iota(jnp.int32, (ts, ts), 0)
    kcol = lax.broadcasted_iota(jnp.int32, (ts, ts), 1)

    def sub_tile(st, _):
        r0 = pl.multiple_of(st * ts, ts)
        qs0 = q0 + r0
        diag = qs0 // ts
        q = q_ref[0, pl.ds(r0, ts), :]
        zq = jnp.zeros_like(q)
        q2 = jnp.concatenate([jnp.where(head0, q, zq), jnp.where(head0, zq, q)], axis=0)
        edge = lambda ks: (rel < (qs0 - ks)) & ((ks + kcol) >= lead)

        def step(ks, carry, nb, masked):
            acc, run = carry
            zs = [_dot_nt(q2, k_ref[0, pl.ds(_aligned(ks + g * ts, ts), min(2, nb - g) * ts), :])
                  for g in range(0, nb, 2)]
            a_blocks = [None] * nb
            for kk in reversed(range(nb)):
                zz = zs[kk // 2][:, (kk % 2) * ts:(kk % 2 + 1) * ts]
                ls = _log_sigmoid(zz)
                lr = ls - zz
                if masked:
                    vis = edge(ks + kk * ts)
                    vis2 = jnp.concatenate([vis, vis], axis=0)
                    lr = jnp.where(vis2, lr, 0.0)
                hi, lo = _split2(lr)
                s2 = _dot(jnp.concatenate([hi, lo], axis=1), u2_ref[...])
                a = jnp.exp(ls + s2[:, :ts] + run)
                if masked:
                    a = jnp.where(vis2, a, 0.0)
                a_blocks[kk] = a.astype(BF16)
                run = run + s2[:, ts:]
            for g in range(0, nb, 2):
                grp = a_blocks[g:g + 2]
                a2 = grp[0] if len(grp) == 1 else jnp.concatenate(grp, axis=1)
                acc = acc + _dot(a2, v_ref[0, pl.ds(_aligned(ks + g * ts, ts), len(grp) * ts), :])
            return acc, run

        zero = jnp.zeros((2 * ts, LANES), F32)
        blk = lambda kb: pl.multiple_of(kb * ts, ts)
        carry = step(blk(diag), (zero, zero), 1, True)
        n_mid = jnp.maximum(diag - 1, 0)
        n4 = lax.shift_right_logical(n_mid, 2)
        carry = lax.fori_loop(0, n4, lambda j, c: step(blk(diag - 4 - 4 * j), c, 4, False), carry)
        carry = _loop_if((n_mid & 2) != 0, lambda c: step(blk(1 + (n_mid & 1)), c, 2, False), carry)
        carry = _loop_if((n_mid & 1) != 0, lambda c: step(blk(1), c, 1, False), carry)
        carry = _loop_if(diag >= 1, lambda c: step(0, c, 1, True), carry)
        acc = carry[0]
        o_ref[0, pl.ds(r0, ts), :] = jnp.where(head0, acc[:ts], acc[ts:]).astype(o_ref.dtype)
        return 0

    lax.fori_loop(0, tq // ts, sub_tile, 0)


def _aligned(x, m):
    return x if isinstance(x, int) else pl.multiple_of(x, m)


def _loop_if(pred, f, carry):
    return lax.fori_loop(0, pred.astype(jnp.int32), lambda j, c: f(c), carry)


def _suffix_matrix(tk):
    j = jnp.arange(tk)[:, None]
    s = jnp.arange(tk)[None, :]
    u = jnp.concatenate([(j > s).astype(BF16), jnp.ones((tk, tk), BF16)], axis=1)
    return jnp.concatenate([u, u], axis=0)


def sb_attention(q, k, v, *, tq, lead):
    b, lp, dq = q.shape
    ts = ATTN_BLOCK
    u2 = _suffix_matrix(ts)
    return pl.pallas_call(
        functools.partial(_sb_attn_kernel, tq=tq, ts=ts, lead=lead),
        grid=(b, dq // LANES, lp // tq),
        in_specs=[
            pl.BlockSpec((1, tq, LANES), lambda i, p, j: (i, j, p)),
            pl.BlockSpec((1, lp, LANES), lambda i, p, j: (i, 0, p)),
            pl.BlockSpec((1, lp, LANES), lambda i, p, j: (i, 0, p)),
            pl.BlockSpec(u2.shape, lambda i, p, j: (0, 0)),
        ],
        out_specs=pl.BlockSpec((1, tq, LANES), lambda i, p, j: (i, j, p)),
        out_shape=jax.ShapeDtypeStruct((b, lp, dq), BF16),
        compiler_params=_params(("parallel", "parallel", "arbitrary")),
        name="sb_attention",
    )(q, k, v, u2)


def _fox_attn_kernel(q_ref, k_ref, v_ref, cq_ref, ckt_ref, o_ref, *, tq, ts, lead):
    pair = pl.program_id(1)
    q0 = pl.program_id(2) * tq
    lane = lax.broadcasted_iota(jnp.int32, (1, LANES), 1)
    head0 = lane < HEAD_DIM
    rel = lax.broadcasted_iota(jnp.int32, (ts, ts), 1) - lax.broadcasted_iota(jnp.int32, (ts, ts), 0)
    kcol = lax.broadcasted_iota(jnp.int32, (ts, ts), 1)

    def sub_tile(st, _):
        r0 = pl.multiple_of(st * ts, ts)
        qs0 = q0 + r0
        diag = qs0 // ts
        q = q_ref[0, pl.ds(r0, ts), :]
        cq_all = cq_ref[0, pl.ds(r0, ts), :]
        zq = jnp.zeros_like(q)
        q2 = jnp.concatenate([jnp.where(head0, q, zq), jnp.where(head0, zq, q)], axis=0)
        cq2 = jnp.concatenate([jnp.broadcast_to(
            jnp.sum(jnp.where(lane == pair * 2 + hh, cq_all, 0.0), axis=1, keepdims=True), (ts, ts))
            for hh in range(2)], axis=0)
        edge = lambda ks: (rel <= (qs0 - ks)) & ((ks + kcol) >= lead)

        def scores(ks, nb, bias, masked):
            zs = [_dot_nt(q2, k_ref[0, pl.ds(_aligned(ks + g * ts, ts), min(2, nb - g) * ts), :]) for g in range(0, nb, 2)]
            tiles = []
            for kk in range(nb):
                ck = jnp.concatenate(
                    [jnp.broadcast_to(ckt_ref[0, hh, :, pl.ds(_aligned(ks + kk * ts, ts), ts)], (ts, ts)) for hh in range(2)],
                    axis=0)
                t = zs[kk // 2][:, (kk % 2) * ts:(kk % 2 + 1) * ts] + bias - ck
                if masked:
                    vis = edge(ks + kk * ts)
                    t = jnp.where(jnp.concatenate([vis, vis], axis=0), t, NEG_INF)
                tiles.append(t)
            return tiles

        blk = lambda kb: pl.multiple_of(kb * ts, ts)
        n_mid = jnp.maximum(diag - 1, 0)
        n4 = lax.shift_right_logical(n_mid, 2)

        def sweep(f, carry):
            carry = f(0, carry, 1, True)
            carry = lax.fori_loop(0, n4, lambda j, c: f(blk(1 + 4 * j), c, 4, False), carry)
            carry = _loop_if((n_mid & 2) != 0, lambda c: f(blk(1 + 4 * n4), c, 2, False), carry)
            carry = _loop_if((n_mid & 1) != 0, lambda c: f(blk(diag - 1), c, 1, False), carry)
            return _loop_if(diag >= 1, lambda c: f(blk(diag), c, 1, True), carry)

        def row_max(ks, mx, nb, masked):
            for t in scores(ks, nb, cq2, masked):
                mx = jnp.maximum(mx, t)
            return mx

        mx = sweep(row_max, jnp.full((2 * ts, ts), NEG_INF, F32))
        m2 = jnp.broadcast_to(jnp.max(mx, axis=1, keepdims=True), (2 * ts, ts))
        cm2 = cq2 - m2

        def accumulate(ks, carry, nb, masked):
            acc, lsum = carry
            if masked:
                ps = [jnp.exp(t - m2) for t in scores(ks, nb, cq2, True)]
            else:
                ps = [jnp.exp(t) for t in scores(ks, nb, cm2, False)]
            for p in ps:
                lsum = lsum + p
            for g in range(0, nb, 2):
                grp = [p.astype(BF16) for p in ps[g:g + 2]]
                p2 = grp[0] if len(grp) == 1 else jnp.concatenate(grp, axis=1)
                acc = acc + _dot(p2, v_ref[0, pl.ds(_aligned(ks + g * ts, ts), len(grp) * ts), :])
            return acc, lsum

        zero = jnp.zeros((2 * ts, LANES), F32)
        acc, lsum = sweep(accumulate, (zero, zero))
        out = acc / jnp.sum(lsum, axis=1, keepdims=True)
        o_ref[0, pl.ds(r0, ts), :] = jnp.where(head0, out[:ts], out[ts:]).astype(o_ref.dtype)
        return 0

    lax.fori_loop(0, tq // ts, sub_tile, 0)


def fox_attention(q, k, v, cum, cum_t, *, tq, lead):
    b, lp, dq = q.shape
    return pl.pallas_call(
        functools.partial(_fox_attn_kernel, tq=tq, ts=ATTN_BLOCK, lead=lead),
        grid=(b, dq // LANES, lp // tq),
        in_specs=[
            pl.BlockSpec((1, tq, LANES), lambda i, p, j: (i, j, p)),
            pl.BlockSpec((1, lp, LANES), lambda i, p, j: (i, 0, p)),
            pl.BlockSpec((1, lp, LANES), lambda i, p, j: (i, 0, p)),
            pl.BlockSpec((1, tq, LANES), lambda i, p, j: (i, j, 0)),
            pl.BlockSpec((1, 2, 1, lp), lambda i, p, j: (i, p, 0, 0)),
        ],
        out_specs=pl.BlockSpec((1, tq, LANES), lambda i, p, j: (i, j, p)),
        out_shape=jax.ShapeDtypeStruct((b, lp, dq), BF16),
        compiler_params=_params(("parallel", "parallel", "arbitrary")),
        name="fox_attention",
    )(q, k, v, cum, cum_t)


def _logf_cum_kernel(fl_ref, bf_ref, logf_ref, cum_ref, cumt_ref, carry_ref, carryt_ref, *, lead):
    j = pl.program_id(1)

    @pl.when(j == 0)
    def _():
        carry_ref[...] = jnp.zeros_like(carry_ref)
        carryt_ref[...] = jnp.zeros_like(carryt_ref)

    x = fl_ref[0] + bf_ref[...]
    logf = jnp.minimum(x, 0.0) - jnp.log1p(jnp.exp(-jnp.abs(x)))
    logf_ref[0] = logf
    tb = logf.shape[0]
    pos = j * tb + lax.broadcasted_iota(jnp.int32, (tb, 1), 0)
    lz = jnp.where(pos >= lead, logf, 0.0)
    ri = lax.broadcasted_iota(jnp.int32, (tb, tb), 0)
    ci = lax.broadcasted_iota(jnp.int32, (tb, tb), 1)
    tril = (ci <= ri).astype(BF16)
    triu = (ri <= ci).astype(BF16)
    c = carry_ref[...]
    ct = carryt_ref[...]
    for p in _split3(lz):
        c = c + _dot(tril, p)
        ct = ct + _dot_tn(p, triu)
    cum_ref[0] = c
    cumt_ref[0] = ct
    carry_ref[...] = jnp.broadcast_to(c[tb - 1:tb, :], carry_ref.shape)
    carryt_ref[...] = jnp.broadcast_to(ct[:, tb - 1:tb], carryt_ref.shape)


def logf_cumsum(fl, b_f, *, lead):
    b, lp, _ = fl.shape
    tb = LANES
    return pl.pallas_call(
        functools.partial(_logf_cum_kernel, lead=lead),
        grid=(b, lp // tb),
        in_specs=[
            pl.BlockSpec((1, tb, LANES), lambda i, j: (i, j, 0)),
            pl.BlockSpec((1, LANES), lambda i, j: (0, 0)),
        ],
        out_specs=[
            pl.BlockSpec((1, tb, LANES), lambda i, j: (i, j, 0)),
            pl.BlockSpec((1, tb, LANES), lambda i, j: (i, j, 0)),
            pl.BlockSpec((1, LANES, tb), lambda i, j: (i, 0, j)),
        ],
        out_shape=[
            jax.ShapeDtypeStruct((b, lp, LANES), F32),
            jax.ShapeDtypeStruct((b, lp, LANES), F32),
            jax.ShapeDtypeStruct((b, LANES, lp), F32),
        ],
        scratch_shapes=[pltpu.VMEM((tb, LANES), F32), pltpu.VMEM((LANES, tb), F32)],
        compiler_params=_params(("parallel", "arbitrary")),
        name="logf_cumsum",
    )(fl, b_f)


def _pair_masks(rows):
    lane = lax.broadcasted_iota(jnp.int32, (rows, LANES), 1)
    return lane < HEAD_DIM, lane >= HEAD_DIM


def _bd(x):
    m0, m1 = _pair_masks(x.shape[0])
    zero = jnp.zeros_like(x)
    return jnp.concatenate([jnp.where(m0, x, zero), jnp.where(m1, x, zero)], axis=0)


def _rwkv_pair_chunk(r, k, v, av, b, ld, bds):
    c = r.shape[0]
    ri = lax.broadcasted_iota(jnp.int32, (c, c), 0)
    ci = lax.broadcasted_iota(jnp.int32, (c, c), 1)
    tril = (ci <= ri).astype(BF16)
    ones = jnp.ones((c, LANES), BF16)
    cs = jnp.zeros((c, LANES), F32)
    tot = jnp.zeros((LANES, LANES), F32)
    for part in _split3(ld):
        cs = cs + _dot(tril, part)
        tot = tot + _dot_tn(part, ones)
    e_pos = jnp.exp(cs)
    e_neg = jnp.exp(-cs)
    at = (av * jnp.exp(cs - ld)).astype(BF16)
    rt = (r * e_pos).astype(BF16)
    bt = b * e_neg
    kt = k * e_neg
    e_end = jnp.exp(cs[c - 1:c, :] - cs)
    bh = (b * e_end).astype(BF16)
    kh = (k * e_end).astype(BF16)

    lhs = jnp.concatenate([at, rt], axis=0)
    rhs = jnp.concatenate([_bd(bt), _bd(kt)], axis=0).astype(BF16)
    sc = _dot_nt(lhs, rhs)
    t2 = lax.broadcasted_iota(jnp.int32, (c, 2 * c), 0)
    s2 = lax.broadcasted_iota(jnp.int32, (c, 2 * c), 1) & (c - 1)
    strict = s2 < t2
    incl = s2 <= t2
    a_ab = jnp.where(strict, sc[:c, :2 * c], 0.0)
    a_ak = jnp.where(strict, sc[:c, 2 * c:], 0.0)
    a_rb = jnp.where(incl, sc[c:, :2 * c], 0.0)
    a_rk = jnp.where(incl, sc[c:, 2 * c:], 0.0)

    inv = jnp.where(s2 == t2, 1.0, 0.0) + a_ab
    lp = a_ab
    n_sq = max(c.bit_length() - 2, 0)
    for _ in range(n_sq):
        lp = _dot(lp.astype(BF16), _bd(lp).astype(BF16))
        inv = inv + _dot(inv.astype(BF16), _bd(lp).astype(BF16))

    bds16 = bds.astype(BF16)
    vbd = _bd(v).astype(BF16)
    rhs_u = _dot(at, bds16) + _dot(a_ak.astype(BF16), vbd)
    u = _dot(inv.astype(BF16), _bd(rhs_u).astype(BF16))
    ubd = _bd(u).astype(BF16)
    y = (_dot(rt, bds16)
         + _dot(jnp.concatenate([a_rb, a_rk], axis=1).astype(BF16), jnp.concatenate([ubd, vbd], axis=0)))
    cross = _dot_tn(bh, u.astype(BF16)) + _dot_tn(kh, v.astype(BF16))
    bds_new = jnp.exp(tot) * bds + jnp.where(_same_head(LANES, LANES), cross, 0.0)
    return y, bds_new


def _same_head(rows, cols):
    rr = lax.broadcasted_iota(jnp.int32, (rows, cols), 0) >= HEAD_DIM
    cc = lax.broadcasted_iota(jnp.int32, (rows, cols), 1) >= HEAD_DIM
    return rr == cc


def _head_sum(x, scale=1.0):
    g = jnp.where(_same_head(LANES, LANES), scale, 0.0).astype(BF16)
    hi, lo = _split2(x)
    return _dot(hi, g) + _dot(lo, g)


def _softplus(x):
    return jnp.maximum(x, 0.0) + jnp.log1p(jnp.exp(-jnp.abs(x)))


def _rwkv_prep(ps, w0, w2p, a0, a2p, g2):
    da = w0.shape[1]
    x12 = ps[:, 3 * da:3 * da + LANES]
    lg = ps[:, 3 * da + LANES:3 * da + 2 * LANES]
    w_log = -_softplus(-(w0 + _dot(jnp.tanh(x12).astype(BF16), w2p))) - 0.5
    ld = -jnp.exp(w_log)
    a = jax.nn.sigmoid(a0 + _dot(x12.astype(BF16), a2p))
    g = _dot(jax.nn.sigmoid(lg).astype(BF16), g2)
    return ld, a, g


def _rwkv_pair_inputs(ps, a, kk_w, ka_w, pr):
    da = a.shape[1]
    sl = slice(LANES * pr, LANES * (pr + 1))
    r = ps[:, sl]
    k = ps[:, da + LANES * pr:da + LANES * (pr + 1)]
    v = ps[:, 2 * da + LANES * pr:2 * da + LANES * (pr + 1)]
    kk = k * kk_w[:, sl]
    kk = kk / jnp.maximum(jnp.sqrt(_head_sum(kk * kk)), 1e-12)
    ap = a[:, sl]
    k2 = k * (1.0 + (ap - 1.0) * ka_w[:, sl])
    return r, k2, v, kk, ap


def _rwkv_pair_output(y, r, k2, v, g, rk_w, lnw, lnb, pr):
    sl = slice(LANES * pr, LANES * (pr + 1))
    mean = _head_sum(y, 1.0 / HEAD_DIM)
    d = y - mean
    var = _head_sum(d * d, 1.0 / HEAD_DIM)
    yn = d * lax.rsqrt(var + GN_EPS) * lnw[:, sl] + lnb[:, sl]
    bonus = _head_sum(r * k2 * rk_w[:, sl]) * v
    return (yn + bonus) * g[:, sl]


def _rwkv_mix_kernel(p_ref, mu_ref, w0_ref, w2_ref, a0_ref, a2_ref, g2_ref, kk_ref, ka_ref, rk_ref,
                     lnw_ref, lnb_ref, ya_ref, sfin_ref, prev_ref, st_ref, *, c, n_pairs):
    j = pl.program_id(1)

    @pl.when(j == 0)
    def _():
        prev_ref[...] = jnp.zeros_like(prev_ref)
        st_ref[...] = jnp.zeros_like(st_ref)

    p = p_ref[0]
    row = lax.broadcasted_iota(jnp.int32, (c, 1), 0)
    p_prev = jnp.where(row == 0, prev_ref[0:1, :], pltpu.roll(p, 1, 0))
    prev_ref[...] = jnp.broadcast_to(p[c - 1:c, :], prev_ref.shape)
    ps = p + (p_prev - p) * mu_ref[...]
    ld, a, g = _rwkv_prep(ps, w0_ref[...], w2_ref[...], a0_ref[...], a2_ref[...], g2_ref[...])
    for pr in range(n_pairs):
        sl = slice(LANES * pr, LANES * (pr + 1))
        r, k2, v, kk, ap = _rwkv_pair_inputs(ps, a, kk_ref[...], ka_ref[...], pr)
        y, bds = _rwkv_pair_chunk(r, k2, v, -kk, kk * ap, ld[:, sl], st_ref[pr])
        st_ref[pr] = bds
        out = _rwkv_pair_output(y, r, k2, v, g, rk_ref[...], lnw_ref[...], lnb_ref[...], pr)
        ya_ref[0, :, sl] = out.astype(ya_ref.dtype)

    @pl.when(j == pl.num_programs(1) - 1)
    def _():
        sfin_ref[0] = st_ref[...]


def rwkv_mix_prompt(p, mu, w0, w2p, a0, a2p, g2, kk_w, ka_w, rk_w, lnw, lnb):
    b, lp, ds = p.shape
    da = w0.shape[1]
    n_pairs = da // LANES
    c = RWKV_CHUNK
    vec = lambda n: pl.BlockSpec((1, n), lambda i, j: (0, 0))
    mat = lambda w: pl.BlockSpec(w.shape, lambda i, j: (0, 0))
    return pl.pallas_call(
        functools.partial(_rwkv_mix_kernel, c=c, n_pairs=n_pairs),
        grid=(b, lp // c),
        in_specs=[pl.BlockSpec((1, c, ds), lambda i, j: (i, j, 0)), vec(ds), vec(da), mat(w2p), vec(da), mat(a2p),
                  mat(g2), vec(da), vec(da), vec(da), vec(da), vec(da)],
        out_specs=[pl.BlockSpec((1, c, da), lambda i, j: (i, j, 0)),
                   pl.BlockSpec((1, n_pairs, LANES, LANES), lambda i, j: (i, 0, 0, 0))],
        out_shape=[jax.ShapeDtypeStruct((b, lp, da), BF16),
                   jax.ShapeDtypeStruct((b, n_pairs, LANES, LANES), F32)],
        scratch_shapes=[pltpu.VMEM((8, ds), F32), pltpu.VMEM((n_pairs, LANES, LANES), F32)],
        compiler_params=_params(("parallel", "arbitrary")),
        name="rwkv_mix_prompt",
    )(p, mu, w0, w2p, a0, a2p, g2, kk_w, ka_w, rk_w, lnw, lnb)


def _rwkv_step_kernel(p_ref, prev_ref, s_ref, mu_ref, w0_ref, w2_ref, a0_ref, a2_ref, g2_ref, kk_ref, ka_ref,
                      rk_ref, lnw_ref, lnb_ref, ya_ref, snew_ref, r_s, w_s, k_s, v_s, a_s, b_s, y_s,
                      *, n_pairs):
    p = p_ref[...]
    ps = p + (prev_ref[...] - p) * mu_ref[...]
    ld, a, g = _rwkv_prep(ps, w0_ref[...], w2_ref[...], a0_ref[...], a2_ref[...], g2_ref[...])
    pair_vals = []
    for pr in range(n_pairs):
        sl = slice(LANES * pr, LANES * (pr + 1))
        r, k2, v, kk, ap = _rwkv_pair_inputs(ps, a, kk_ref[...], ka_ref[...], pr)
        pair_vals.append((r, k2, v))
        r_s[:, sl] = r
        w_s[:, sl] = jnp.exp(ld[:, sl])
        k_s[:, sl] = k2
        v_s[:, sl] = v
        a_s[:, sl] = -kk
        b_s[:, sl] = kk * ap

    hd = HEAD_DIM
    eye = lax.broadcasted_iota(jnp.int32, (hd, hd), 0) == lax.broadcasted_iota(jnp.int32, (hd, hd), 1)
    sel = (lax.broadcasted_iota(jnp.int32, (hd, LANES), 1) & (hd - 1)) == lax.broadcasted_iota(
        jnp.int32, (hd, LANES), 0)
    lane = lax.broadcasted_iota(jnp.int32, (1, LANES), 1)

    def body(bg, carry):
        b0 = pl.multiple_of(bg * 8, 8)
        for pr in range(n_pairs):
            sl = slice(LANES * pr, LANES * (pr + 1))
            tiles = {}
            for name, ref in (("a", a_s), ("b", b_s), ("w", w_s), ("k", k_s), ("r", r_s), ("v", v_s)):
                x = ref[pl.ds(b0, 8), sl]
                tiles[name] = (x, pltpu.roll(x, hd, 1))
            y_rows = []
            for i in range(8):
                heads = []
                for hh in range(2):
                    vec = lambda name: tiles[name][hh][i:i + 1, :hd]
                    s = s_ref[b0 + i, 2 * pr + hh]
                    sa = jnp.sum(s * vec("a"), axis=1, keepdims=True)
                    v_col = jnp.sum(jnp.where(eye, vec("v"), 0.0), axis=1, keepdims=True)
                    s2 = s * vec("w") + sa * vec("b") + v_col * vec("k")
                    snew_ref[b0 + i, 2 * pr + hh] = s2
                    y_col = jnp.sum(s2 * vec("r"), axis=1, keepdims=True)
                    heads.append(jnp.sum(jnp.where(sel, y_col, 0.0), axis=0, keepdims=True))
                y_rows.append(jnp.where(lane < hd, heads[0], heads[1]))
            y_s[pl.ds(b0, 8), sl] = jnp.concatenate(y_rows, axis=0)
        return carry

    lax.fori_loop(0, p.shape[0] // 8, body, 0)
    y = y_s[...]
    for pr in range(n_pairs):
        sl = slice(LANES * pr, LANES * (pr + 1))
        r, k2, v = pair_vals[pr]
        out = _rwkv_pair_output(y[:, sl], r, k2, v, g, rk_ref[...], lnw_ref[...], lnb_ref[...], pr)
        ya_ref[:, sl] = out.astype(ya_ref.dtype)


def rwkv_step(p, prev, state, mu, w0, w2p, a0, a2p, g2, kk_w, ka_w, rk_w, lnw, lnb):
    db, ds = p.shape
    da = w0.shape[1]
    return pl.pallas_call(
        functools.partial(_rwkv_step_kernel, n_pairs=da // LANES),
        out_shape=[jax.ShapeDtypeStruct((db, da), BF16), jax.ShapeDtypeStruct(state.shape, F32)],
        scratch_shapes=[pltpu.VMEM((db, da), F32) for _ in range(7)],
        compiler_params=pltpu.CompilerParams(vmem_limit_bytes=VMEM_LIMIT),
        name="rwkv_step",
    )(p, prev, state, mu, w0, w2p, a0, a2p, g2, kk_w, ka_w, rk_w, lnw, lnb)


def _head_rows(q_row, rows):
    width = q_row.shape[1]
    row = lax.broadcasted_iota(jnp.int32, (rows, width), 0)
    col = lax.broadcasted_iota(jnp.int32, (rows, width), 1) >> 6
    return jnp.where(row == col, q_row, 0.0)


def _diag_blocks(acc):
    row = lax.broadcasted_iota(jnp.int32, acc.shape, 0)
    col = lax.broadcasted_iota(jnp.int32, acc.shape, 1) >> 6
    return jnp.sum(jnp.where(row == col, acc, 0.0), axis=0, keepdims=True)


def _log_sigmoid(x):
    return jnp.minimum(x, 0.0) - jnp.log1p(jnp.exp(-jnp.abs(x)))


def _sb_decode_kernel(pt_ref, q_ref, kn_ref, vn_ref, u2_ref, *rest, pp, n_pages, n_heads):
    k_refs, v_refs = rest[:pp], rest[pp:2 * pp]
    o_ref, qt_ref, acc_ref, run_ref = rest[2 * pp:]
    j = pl.program_id(1)
    qpos = n_pages * PAGE_SIZE
    lane = lax.broadcasted_iota(jnp.int32, (1, PAGE_SIZE), 1)

    def page(kpg, vpg, kpos0, n_valid):
        z = _dot(qt_ref[...], kpg.astype(BF16))[:n_heads]
        vis = ((kpos0 + lane) < qpos) & (lane < n_valid)
        ls = _log_sigmoid(z)
        lr = jnp.where(vis, ls - z, 0.0)
        hi, lo = _split2(lr)
        s2 = _dot(jnp.concatenate([hi, lo], axis=1), u2_ref[...])
        a = jnp.where(vis, jnp.exp(ls + s2[:, :PAGE_SIZE] + run_ref[...]), 0.0)
        acc_ref[...] += _dot_nt(a.astype(BF16), vpg.astype(BF16))
        run_ref[...] += s2[:, PAGE_SIZE:]

    @pl.when(j == 0)
    def _():
        qt_ref[...] = _head_rows(q_ref[0].astype(F32), qt_ref.shape[0]).astype(BF16)
        acc_ref[...] = jnp.zeros_like(acc_ref)
        run_ref[...] = jnp.zeros_like(run_ref)
        page(kn_ref[0], vn_ref[0], qpos, 1)

    for s in range(pp):
        page(k_refs[s][0, 0], v_refs[s][0, 0], (n_pages - 1 - (j * pp + s)) * PAGE_SIZE, PAGE_SIZE)

    @pl.when(j == pl.num_programs(1) - 1)
    def _():
        o_ref[0] = _diag_blocks(acc_ref[...]).astype(o_ref.dtype)


def _page_specs(layer, pp, n_pages, width):
    def spec(s):
        return pl.BlockSpec((1, 1, width, PAGE_SIZE),
                            lambda b, j, pt: (layer, pt[b, n_pages - 1 - (j * pp + s)], 0, 0))
    return [spec(s) for s in range(pp)]


def _cache_view(cache):
    nl, n_pool, rows, n_heads, hd = cache.shape
    return jnp.transpose(cache, (0, 1, 3, 4, 2)).reshape(nl, n_pool, n_heads * hd, rows)


def _new_token_page(x):
    return jnp.pad(x[:, :, None], ((0, 0), (0, 0), (0, PAGE_SIZE - 1)))


def sb_decode(q, k_new, v_new, cache_k, cache_v, layer, page_table, *, pp):
    db, _, d = q.shape
    n_pages = page_table.shape[1]
    n_heads = d // HEAD_DIM
    u2 = _suffix_matrix(PAGE_SIZE)
    row = lambda r, w: pl.BlockSpec((1, r, w), lambda b, j, pt: (b, 0, 0))
    grid_spec = pltpu.PrefetchScalarGridSpec(
        num_scalar_prefetch=1,
        grid=(db, n_pages // pp),
        in_specs=[row(1, d), row(d, PAGE_SIZE), row(d, PAGE_SIZE),
                  pl.BlockSpec(u2.shape, lambda b, j, pt: (0, 0))]
                 + _page_specs(layer, pp, n_pages, d) + _page_specs(layer, pp, n_pages, d),
        out_specs=row(1, d),
        scratch_shapes=[pltpu.VMEM((16, d), BF16), pltpu.VMEM((n_heads, d), F32),
                        pltpu.VMEM((n_heads, PAGE_SIZE), F32)],
    )
    return pl.pallas_call(
        functools.partial(_sb_decode_kernel, pp=pp, n_pages=n_pages, n_heads=n_heads),
        grid_spec=grid_spec,
        out_shape=jax.ShapeDtypeStruct((db, 1, d), BF16),
        compiler_params=_params(("parallel", "arbitrary")),
        name="sb_decode",
    )(page_table, q, k_new, v_new, u2, *([cache_k] * pp), *([cache_v] * pp))


def _fox_decode_kernel(pt_ref, q_ref, kn_ref, vn_ref, fl_ref, bf_ref, m2_ref, *rest, pp, n_pages, n_heads):
    k_refs, v_refs, lf_refs = rest[:pp], rest[pp:2 * pp], rest[2 * pp:3 * pp]
    o_ref, lfo_ref, qt_ref, acc_ref, m_ref, l_ref, run_ref = rest[3 * pp:]
    j = pl.program_id(1)
    qpos = n_pages * PAGE_SIZE
    lane = lax.broadcasted_iota(jnp.int32, (1, PAGE_SIZE), 1)

    def page(kpg, vpg, bias, kpos0, n_valid):
        z = _dot(qt_ref[...], kpg.astype(BF16))[:n_heads] + bias
        vis = ((kpos0 + lane) <= qpos) & (lane < n_valid)
        z = jnp.where(vis, z, NEG_INF)
        m_new = jnp.maximum(m_ref[...], jnp.max(z, axis=1, keepdims=True))
        alpha = jnp.exp(m_ref[...] - m_new)
        p = jnp.exp(z - m_new)
        l_ref[...] = alpha * l_ref[...] + jnp.sum(p, axis=1, keepdims=True)
        acc_ref[...] = alpha * acc_ref[...] + _dot_nt(p.astype(BF16), vpg.astype(BF16))
        m_ref[...] = m_new

    @pl.when(j == 0)
    def _():
        qt_ref[...] = _head_rows(q_ref[0].astype(F32), qt_ref.shape[0]).astype(BF16)
        acc_ref[...] = jnp.zeros_like(acc_ref)
        m_ref[...] = jnp.full_like(m_ref, NEG_INF)
        l_ref[...] = jnp.zeros_like(l_ref)
        logf = _log_sigmoid(fl_ref[0] + bf_ref[...])
        lfo_ref[0] = logf
        eye = lax.broadcasted_iota(jnp.int32, (LANES, LANES), 0) == lax.broadcasted_iota(
            jnp.int32, (LANES, LANES), 1)
        col = jnp.sum(jnp.where(eye, logf, 0.0), axis=1, keepdims=True)
        run_ref[...] = jnp.broadcast_to(col[:n_heads], run_ref.shape)
        page(kn_ref[0], vn_ref[0], 0.0, qpos, 1)

    for s in range(pp):
        lf = lf_refs[s][0]
        s2 = jnp.zeros((n_heads, 2 * PAGE_SIZE), F32)
        for part in _split3(lf):
            s2 = s2 + _dot(part, m2_ref[...])
        page(k_refs[s][0, 0], v_refs[s][0, 0], run_ref[...] + s2[:, :PAGE_SIZE],
             (n_pages - 1 - (j * pp + s)) * PAGE_SIZE, PAGE_SIZE)
        run_ref[...] += s2[:, PAGE_SIZE:]

    @pl.when(j == pl.num_programs(1) - 1)
    def _():
        o_ref[0] = _diag_blocks(acc_ref[...] / l_ref[...]).astype(o_ref.dtype)


def fox_decode(q, k_new, v_new, fl_new, b_f, cache_k, cache_v, layer, cache_lf_t, page_table, *, pp):
    db, _, d = q.shape
    n_pages = page_table.shape[1]
    n_heads = d // HEAD_DIM
    jj = jnp.arange(PAGE_SIZE)[:, None]
    ss = jnp.arange(PAGE_SIZE)[None, :]
    m2 = jnp.concatenate([(jj > ss).astype(BF16), jnp.ones((PAGE_SIZE, PAGE_SIZE), BF16)], axis=1)
    row = lambda r, w: pl.BlockSpec((1, r, w), lambda b, j, pt: (b, 0, 0))
    lf_specs = [pl.BlockSpec((1, n_heads, PAGE_SIZE),
                             lambda b, j, pt, s=s: (pt[b, n_pages - 1 - (j * pp + s)], 0, 0)) for s in range(pp)]
    grid_spec = pltpu.PrefetchScalarGridSpec(
        num_scalar_prefetch=1,
        grid=(db, n_pages // pp),
        in_specs=[row(1, d), row(d, PAGE_SIZE), row(d, PAGE_SIZE), row(1, LANES),
                  pl.BlockSpec((1, LANES), lambda b, j, pt: (0, 0)),
                  pl.BlockSpec(m2.shape, lambda b, j, pt: (0, 0))]
                 + _page_specs(layer, pp, n_pages, d) + _page_specs(layer, pp, n_pages, d) + lf_specs,
        out_specs=[row(1, d), row(1, LANES)],
        scratch_shapes=[pltpu.VMEM((16, d), BF16), pltpu.VMEM((n_heads, d), F32),
                        pltpu.VMEM((n_heads, 1), F32), pltpu.VMEM((n_heads, 1), F32),
                        pltpu.VMEM((n_heads, PAGE_SIZE), F32)],
    )
    return pl.pallas_call(
        functools.partial(_fox_decode_kernel, pp=pp, n_pages=n_pages, n_heads=n_heads),
        grid_spec=grid_spec,
        out_shape=[jax.ShapeDtypeStruct((db, 1, d), BF16), jax.ShapeDtypeStruct((db, 1, LANES), F32)],
        compiler_params=_params(("parallel", "arbitrary")),
        name="fox_decode",
    )(page_table, q, k_new, v_new, fl_new, b_f, m2, *([cache_k] * pp), *([cache_v] * pp), *([cache_lf_t] * pp))


def _strided_suffix(x, stride):
    n = x.shape[1]
    lane = lax.broadcasted_iota(jnp.int32, (1, n), 1)
    sh = stride
    while sh < n:
        x = x + jnp.where(lane < n - sh, pltpu.roll(x, n - sh, 1), 0.0)
        sh *= 2
    return x


def _lane_tile(x, stride):
    n = x.shape[1]
    lane = lax.broadcasted_iota(jnp.int32, (1, n), 1)
    t = jnp.where(lane < stride, x, 0.0)
    sh = stride
    while sh < n:
        t = t + pltpu.roll(t, sh, 1)
        sh *= 2
    return t


def _head_match(n_heads, n):
    lane = lax.broadcasted_iota(jnp.int32, (n_heads, n), 1)
    row = lax.broadcasted_iota(jnp.int32, (n_heads, n), 0)
    shift = n_heads.bit_length() - 1
    return (lane & (n_heads - 1)) == row, lane >> shift


def _query_rows(q_ref):
    q = q_ref[0]
    if q.shape[0] < 16:
        q = jnp.concatenate([q, jnp.zeros((16 - q.shape[0], q.shape[1]), F32)], axis=0)
    return q.astype(BF16)


def _sb_decode_kernel2(pt_ref, q_ref, kn_ref, vn_ref, *rest, pp, n_pages, n_heads):
    k_refs, v_refs = rest[:pp], rest[pp:2 * pp]
    o_ref, acc_ref, run_ref = rest[2 * pp:]
    j = pl.program_id(1)
    qpos = n_pages * PAGE_SIZE
    qh = _query_rows(q_ref)

    def page(state, kpg, vpg, kpos0, n_valid):
        acc, run = state
        n = kpg.shape[0] * n_heads
        kf = kpg.reshape(n, HEAD_DIM).astype(BF16)
        vf = vpg.reshape(n, HEAD_DIM).astype(BF16)
        match, srow = _head_match(n_heads, n)
        z = _dot_nt(qh, kf)[:n_heads]
        vis = match & ((kpos0 + srow) < qpos) & (srow < n_valid)
        ls = _log_sigmoid(z)
        lr = jnp.where(vis, ls - z, 0.0)
        incl = _strided_suffix(lr, n_heads)
        a = jnp.where(vis, jnp.exp(ls + (incl - lr) + run[:, :n]), 0.0)
        tot = jnp.sum(jnp.where(match & (srow == 0), incl, 0.0), axis=0, keepdims=True)
        if n < run.shape[1]:
            tot = jnp.concatenate([tot, jnp.zeros((1, run.shape[1] - n), F32)], axis=1)
        return acc + _dot(a.astype(BF16), vf), run + _lane_tile(tot, n_heads)

    @pl.when(j == 0)
    def _():
        zero = (jnp.zeros(acc_ref.shape, F32), jnp.zeros(run_ref.shape, F32))
        acc_ref[...], run_ref[...] = page(zero, kn_ref[0], vn_ref[0], qpos, 1)

    state = (acc_ref[...], run_ref[...])
    for s in range(pp):
        state = page(state, k_refs[s][0, 0], v_refs[s][0, 0], (n_pages - 1 - (j * pp + s)) * PAGE_SIZE, PAGE_SIZE)
    acc_ref[...], run_ref[...] = state

    @pl.when(j == pl.num_programs(1) - 1)
    def _():
        o_ref[0] = state[0].astype(o_ref.dtype)


def _cache_page_specs(layer, pp, n_pages, n_heads):
    return [pl.BlockSpec((1, 1, PAGE_SIZE, n_heads, HEAD_DIM),
                         lambda b, j, pt, s=s: (layer, pt[b, n_pages - 1 - (j * pp + s)], 0, 0, 0))
            for s in range(pp)]


def sb_decode_paged(q, k_new, v_new, cache_k, cache_v, layer, page_table, *, pp):
    db, n_heads, _ = q.shape
    n_pages = page_table.shape[1]
    new_rows = k_new.shape[1]
    per_b = lambda shape: pl.BlockSpec((1,) + shape, lambda b, j, pt: (b,) + (0,) * len(shape))
    grid_spec = pltpu.PrefetchScalarGridSpec(
        num_scalar_prefetch=1,
        grid=(db, n_pages // pp),
        in_specs=[per_b((n_heads, HEAD_DIM)), per_b((new_rows, n_heads, HEAD_DIM)),
                  per_b((new_rows, n_heads, HEAD_DIM))]
                 + _cache_page_specs(layer, pp, n_pages, n_heads) + _cache_page_specs(layer, pp, n_pages, n_heads),
        out_specs=per_b((n_heads, HEAD_DIM)),
        scratch_shapes=[pltpu.VMEM((n_heads, HEAD_DIM), F32), pltpu.VMEM((1, PAGE_SIZE * n_heads), F32)],
    )
    return pl.pallas_call(
        functools.partial(_sb_decode_kernel2, pp=pp, n_pages=n_pages, n_heads=n_heads),
        grid_spec=grid_spec,
        out_shape=jax.ShapeDtypeStruct((db, n_heads, HEAD_DIM), BF16),
        compiler_params=_params(("parallel", "arbitrary")),
        name="sb_decode",
    )(page_table, q, k_new, v_new, *([cache_k] * pp), *([cache_v] * pp))


def _fox_decode_kernel2(pt_ref, q_ref, kn_ref, vn_ref, fl_ref, bf_ref, *rest, pp, n_pages, n_heads):
    k_refs, v_refs, lf_refs = rest[:pp], rest[pp:2 * pp], rest[2 * pp:3 * pp]
    o_ref, lfo_ref, acc_ref, m_ref, l_ref, run_ref = rest[3 * pp:]
    j = pl.program_id(1)
    qpos = n_pages * PAGE_SIZE
    qh = _query_rows(q_ref)

    def page(state, kpg, vpg, bias, kpos0, n_valid):
        acc, m, l = state
        n = kpg.shape[0] * n_heads
        kf = kpg.reshape(n, HEAD_DIM).astype(BF16)
        vf = vpg.reshape(n, HEAD_DIM).astype(BF16)
        match, srow = _head_match(n_heads, n)
        z = _dot_nt(qh, kf)[:n_heads] + bias
        vis = match & ((kpos0 + srow) <= qpos) & (srow < n_valid)
        z = jnp.where(vis, z, NEG_INF)
        m_new = jnp.maximum(m, jnp.max(z, axis=1, keepdims=True))
        alpha = jnp.exp(m - m_new)
        p = jnp.where(vis, jnp.exp(z - m_new), 0.0)
        return (alpha * acc + _dot(p.astype(BF16), vf), m_new,
                alpha * l + jnp.sum(p, axis=1, keepdims=True))

    @pl.when(j == 0)
    def _():
        logf = _log_sigmoid(fl_ref[0] + bf_ref[...])
        lfo_ref[0] = logf
        wide = jnp.concatenate([logf, jnp.zeros((1, run_ref.shape[1] - LANES), F32)], axis=1)
        run_ref[...] = _lane_tile(wide, n_heads)
        init = (jnp.zeros(acc_ref.shape, F32), jnp.full(m_ref.shape, NEG_INF, F32), jnp.zeros(l_ref.shape, F32))
        acc_ref[...], m_ref[...], l_ref[...] = page(init, kn_ref[0], vn_ref[0], 0.0, qpos, 1)

    state = (acc_ref[...], m_ref[...], l_ref[...])
    run = run_ref[...]
    for s in range(pp):
        lf = lf_refs[s][0]
        incl = _strided_suffix(lf, n_heads)
        state = page(state, k_refs[s][0, 0], v_refs[s][0, 0], run + (incl - lf),
                     (n_pages - 1 - (j * pp + s)) * PAGE_SIZE, PAGE_SIZE)
        run = run + _lane_tile(incl, n_heads)
    acc_ref[...], m_ref[...], l_ref[...] = state
    run_ref[...] = run

    @pl.when(j == pl.num_programs(1) - 1)
    def _():
        o_ref[0] = (state[0] / state[2]).astype(o_ref.dtype)


def fox_decode_paged(q, k_new, v_new, fl_new, b_f, cache_k, cache_v, layer, cache_lf_rows, page_table, *, pp):
    db, n_heads, _ = q.shape
    n_pages = page_table.shape[1]
    new_rows = k_new.shape[1]
    n = PAGE_SIZE * n_heads
    per_b = lambda shape: pl.BlockSpec((1,) + shape, lambda b, j, pt: (b,) + (0,) * len(shape))
    lf_specs = [pl.BlockSpec((1, 1, n), lambda b, j, pt, s=s: (pt[b, n_pages - 1 - (j * pp + s)], 0, 0))
                for s in range(pp)]
    grid_spec = pltpu.PrefetchScalarGridSpec(
        num_scalar_prefetch=1,
        grid=(db, n_pages // pp),
        in_specs=[per_b((n_heads, HEAD_DIM)), per_b((new_rows, n_heads, HEAD_DIM)),
                  per_b((new_rows, n_heads, HEAD_DIM)), per_b((1, LANES)),
                  pl.BlockSpec((1, LANES), lambda b, j, pt: (0, 0))]
                 + _cache_page_specs(layer, pp, n_pages, n_heads) + _cache_page_specs(layer, pp, n_pages, n_heads)
                 + lf_specs,
        out_specs=[per_b((n_heads, HEAD_DIM)), per_b((1, LANES))],
        scratch_shapes=[pltpu.VMEM((n_heads, HEAD_DIM), F32), pltpu.VMEM((n_heads, 1), F32),
                        pltpu.VMEM((n_heads, 1), F32), pltpu.VMEM((1, n), F32)],
    )
    return pl.pallas_call(
        functools.partial(_fox_decode_kernel2, pp=pp, n_pages=n_pages, n_heads=n_heads),
        grid_spec=grid_spec,
        out_shape=[jax.ShapeDtypeStruct((db, n_heads, HEAD_DIM), BF16), jax.ShapeDtypeStruct((db, 1, LANES), F32)],
        compiler_params=_params(("parallel", "arbitrary")),
        name="fox_decode",
    )(page_table, q, k_new, v_new, fl_new, b_f, *([cache_k] * pp), *([cache_v] * pp), *([cache_lf_rows] * pp))


def _new_token_rows(x, n_heads):
    db = x.shape[0]
    x4 = x.reshape(db, 1, n_heads, HEAD_DIM)
    return jnp.pad(x4, ((0, 0), (0, LANES // n_heads - 1), (0, 0), (0, 0)))


def _pad_lanes(x):
    return jnp.pad(x, [(0, 0)] * (x.ndim - 1) + [(0, LANES - x.shape[-1])])


def kernel(x_prompt, x_sample, cache_sb_k, cache_sb_v, cache_fox_k, cache_fox_v, cache_fox_logf, state_rwkv, state_rwkv_shift, page_table, meta_tokens, norm_g, ffn_w1, ffn_w3, ffn_w2, w_in_even, w_out_even, rwkv_mu, rwkv_w0, rwkv_w2, rwkv_a0, rwkv_a2, rwkv_g2, rwkv_kk, rwkv_ka, rwkv_rk, rwkv_lnw, rwkv_lnb, w_in_odd, b_f, w_out_odd):
    bsz, seq, d = x_prompt.shape
    db, ts, _ = x_sample.shape
    assert ts == 1, "the decode kernels handle one new token per sequence"
    depth = norm_g.shape[0]
    t_p = N_META + seq
    lp = LEAD + t_p
    n_pool, n_pages = cache_sb_k.shape[1], page_table.shape[1]
    h_a = state_rwkv.shape[2]
    d_a = h_a * HEAD_DIM
    d_b = cache_sb_k.shape[3] * HEAD_DIM
    d_c = cache_fox_k.shape[3] * HEAD_DIM
    h_c = d_c // HEAD_DIM
    d_shift = state_rwkv_shift.shape[2]
    dff = ffn_w1.shape[3]

    hp = jnp.concatenate([jnp.zeros((bsz, LEAD, d), F32),
                          jnp.broadcast_to(meta_tokens[None].astype(F32), (bsz, N_META, d)), x_prompt], axis=1)
    hs = x_sample.reshape(db, d)

    tm = _row_tile(lp)
    tq = tm
    tf = dff // 2 if (dff // 2) % LANES == 0 else dff
    pp = min(8, n_pages)
    gvec = lambda l, i: norm_g[l, i][None, :]

    def ffn(h2, l, i):
        w1, w3, w2 = (w[l, i].astype(BF16) for w in (ffn_w1, ffn_w3, ffn_w2))
        return ffn_half(h2, gvec(l, 2 * i * 2), gvec(l, 2 * i * 2 + 1), w1, w3, w2,
                        tm=_row_tile(h2.shape[0]), tf=tf)

    outs_p = {k: [] for k in ("sb_k", "sb_v", "fk", "fv", "fl", "rw", "sh")}
    outs_s = {k: [] for k in ("sb_k", "sb_v", "fk", "fv", "fl", "rw", "sh")}

    for l in range(depth):
        hp = ffn(hp.reshape(bsz * lp, d), l, 0).reshape(bsz, lp, d)
        hs = ffn(hs, l, 0)
        g_mix, g_post = gvec(l, 2), gvec(l, 3)
        if l % 2 == 0:
            e = l // 2
            w_in = w_in_even[e].astype(BF16)
            slabs = [w_in[:, :d_shift], w_in[:, d_shift:d_shift + d_b],
                     w_in[:, d_shift + d_b:d_shift + 2 * d_b], w_in[:, d_shift + 2 * d_b:]]
            dts = [(F32,), (BF16,), (F32, BF16), (F32, BF16)]
            scl = [1.0, HEAD_DIM ** -0.5, 1.0, 1.0]
            zeros_l = jnp.zeros((LORA_W, d_a), F32)
            rw = (rwkv_mu[e][None], rwkv_w0[e][None],
                  jnp.concatenate([rwkv_w2[e], zeros_l], axis=0).astype(BF16), rwkv_a0[e][None],
                  jnp.concatenate([zeros_l, rwkv_a2[e]], axis=0).astype(BF16), rwkv_g2[e].astype(BF16),
                  rwkv_kk[e][None], rwkv_ka[e][None], rwkv_rk[e].reshape(1, d_a), rwkv_lnw[e][None],
                  rwkv_lnb[e][None])
            w_out = w_out_even[e].astype(BF16)
            p, q, k, kb, v, vb = norm_proj(hp.reshape(bsz * lp, d), g_mix, slabs, dts, scl, tm=tm)
            r3 = lambda x: x.reshape(bsz, lp, x.shape[-1])
            ya, sfin = rwkv_mix_prompt(r3(p), *rw)
            yb = sb_attention(r3(q), r3(kb), r3(vb), tq=tq, lead=LEAD)
            hp = out_proj(hp, g_post, [ya, yb], [w_out[:d_a], w_out[d_a:]], tm=tm, lead=LEAD)
            outs_p["sb_k"].append(r3(k)[:, LEAD:].reshape(bsz, t_p, -1, HEAD_DIM))
            outs_p["sb_v"].append(r3(v)[:, LEAD:].reshape(bsz, t_p, -1, HEAD_DIM))
            s_pairs = sfin.reshape(bsz, d_a // LANES, 2, HEAD_DIM, 2, HEAD_DIM)
            s_heads = jnp.stack([s_pairs[:, :, 0, :, 0, :], s_pairs[:, :, 1, :, 1, :]], axis=2)
            outs_p["rw"].append(jnp.swapaxes(s_heads.reshape(bsz, h_a, HEAD_DIM, HEAD_DIM), -1, -2))
            outs_p["sh"].append(r3(p)[:, -1])
            p, q, k, v = norm_proj(hs, g_mix, slabs, [(F32,), (BF16,), (F32,), (F32,)], scl, tm=_row_tile(db))
            ya, s_new = rwkv_step(p, state_rwkv_shift[e], state_rwkv[e], *rw)
            yb = sb_decode(q[:, None], _new_token_page(k), _new_token_page(v), _cache_view(cache_sb_k),
                           _cache_view(cache_sb_v), e, page_table, pp=pp)
            hs = out_proj(hs[None], g_post, [ya[None], yb.reshape(1, db, d_b)], [w_out[:d_a], w_out[d_a:]],
                          tm=_row_tile(db), lead=0)[0]
            outs_s["sb_k"].append(k.reshape(db, 1, -1, HEAD_DIM))
            outs_s["sb_v"].append(v.reshape(db, 1, -1, HEAD_DIM))
            outs_s["rw"].append(s_new)
            outs_s["sh"].append(p)
        else:
            o = l // 2
            w_in = w_in_odd[o]
            slabs = [w_in[:, :d_c].astype(BF16), w_in[:, d_c:2 * d_c].astype(BF16),
                     w_in[:, 2 * d_c:3 * d_c].astype(BF16), _pad_lanes(w_in[:, 3 * d_c:]).astype(BF16)]
            dts = [(BF16,), (F32, BF16), (F32, BF16), (F32,)]
            scl = [HEAD_DIM ** -0.5, 1.0, 1.0, 1.0]
            bf_row = _pad_lanes(b_f[o][None])
            w_out = w_out_odd[o].astype(BF16)
            q, k, kb, v, vb, fl = norm_proj(hp.reshape(bsz * lp, d), g_mix, slabs, dts, scl, tm=tm)
            r3 = lambda x: x.reshape(bsz, lp, x.shape[-1])
            logf, cum, cum_t = logf_cumsum(r3(fl), bf_row, lead=LEAD)
            yc = fox_attention(r3(q), r3(kb), r3(vb), cum, cum_t[:, :h_c, None, :], tq=tq, lead=LEAD)
            hp = out_proj(hp, g_post, [yc], [w_out], tm=tm, lead=LEAD)
            outs_p["fk"].append(r3(k)[:, LEAD:].reshape(bsz, t_p, h_c, HEAD_DIM))
            outs_p["fv"].append(r3(v)[:, LEAD:].reshape(bsz, t_p, h_c, HEAD_DIM))
            outs_p["fl"].append(logf[:, LEAD:, :h_c])
            q, k, v, fl = norm_proj(hs, g_mix, slabs, [(BF16,), (F32,), (F32,), (F32,)], scl, tm=_row_tile(db))
            yc, logf_s = fox_decode(q[:, None], _new_token_page(k), _new_token_page(v), fl[:, None], bf_row,
                                    _cache_view(cache_fox_k), _cache_view(cache_fox_v), o,
                                    jnp.swapaxes(cache_fox_logf[o], 1, 2), page_table, pp=pp)
            hs = out_proj(hs[None], g_post, [yc.reshape(1, db, d_c)], [w_out], tm=_row_tile(db), lead=0)[0]
            outs_s["fk"].append(k.reshape(db, 1, h_c, HEAD_DIM))
            outs_s["fv"].append(v.reshape(db, 1, h_c, HEAD_DIM))
            outs_s["fl"].append(logf_s[:, :, :h_c])
        hp = ffn(hp.reshape(bsz * lp, d), l, 1).reshape(bsz, lp, d)
        hs = ffn(hs, l, 1)

    y_prompt = hp[:, LEAD + N_META:]
    y_sample = hs.reshape(db, 1, d)
    order = ("sb_k", "sb_v", "fk", "fv", "fl", "rw", "sh")
    return (y_prompt, y_sample) + tuple(jnp.stack(outs_p[k]) for k in order) + tuple(
        jnp.stack(outs_s[k]) for k in order)
```

```python
import functools

import jax
import jax.numpy as jnp
from jax import lax
from jax.experimental import pallas as pl
from jax.experimental.pallas import tpu as pltpu

F32 = jnp.float32
BF16 = jnp.bfloat16

HEAD_DIM = 64
N_META = 16
ATTN_BLOCK = 128
LEAD = (-N_META) % ATTN_BLOCK
PAGE_SIZE = 128
LORA_W = 64
LORA_A = 64
LORA_G = 128
FFN_RES = 0.5
NORM_EPS = 1e-6
GN_EPS = 64e-5
NEG_INF = -1e30
LANES = 128
RWKV_CHUNK = 64
VMEM_LIMIT = 52 * 1024 * 1024


def _params(sem):
    return pltpu.CompilerParams(dimension_semantics=sem, vmem_limit_bytes=VMEM_LIMIT)


def _row_tile(m, candidates=(384, 256, 128, 64, 32, 16, 8)):
    for t in candidates:
        if m % t == 0:
            return t
    raise ValueError(f"no row tile for {m}")


def _rms(x, g):
    return x * lax.rsqrt(jnp.mean(x * x, axis=-1, keepdims=True) + NORM_EPS) * g


def _split3(x):
    h = x.astype(BF16)
    r = x - h.astype(F32)
    m = r.astype(BF16)
    l = (r - m.astype(F32)).astype(BF16)
    return h, m, l


def _split2(x):
    h = x.astype(BF16)
    return h, (x - h.astype(F32)).astype(BF16)


def _dot(a, b):
    return jnp.dot(a, b, preferred_element_type=F32)


def _dot_nt(a, b):
    return lax.dot_general(a, b, (((1,), (1,)), ((), ())), preferred_element_type=F32)


def _dot_tn(a, b):
    return lax.dot_general(a, b, (((0,), (0,)), ((), ())), preferred_element_type=F32)


def _ffn_kernel(h_ref, gpre_ref, gpost_ref, w1_ref, w3_ref, w2_ref, o_ref, xn_ref, acc_ref):
    j = pl.program_id(1)

    @pl.when(j == 0)
    def _():
        xn_ref[...] = _rms(h_ref[...], gpre_ref[...]).astype(BF16)
        acc_ref[...] = jnp.zeros_like(acc_ref)

    x = xn_ref[...]
    a = _dot(x, w1_ref[...])
    b = _dot(x, w3_ref[...])
    t = (a * jax.nn.sigmoid(a) * b).astype(BF16)
    acc_ref[...] += _dot(t, w2_ref[...])

    @pl.when(j == pl.num_programs(1) - 1)
    def _():
        o_ref[...] = h_ref[...] + FFN_RES * _rms(acc_ref[...], gpost_ref[...])


def ffn_half(h, g_pre, g_post, w1, w3, w2, *, tm, tf):
    m, d = h.shape
    dff = w1.shape[1]
    return pl.pallas_call(
        _ffn_kernel,
        grid=(m // tm, dff // tf),
        in_specs=[
            pl.BlockSpec((tm, d), lambda i, j: (i, 0)),
            pl.BlockSpec((1, d), lambda i, j: (0, 0)),
            pl.BlockSpec((1, d), lambda i, j: (0, 0)),
            pl.BlockSpec((d, tf), lambda i, j: (0, j)),
            pl.BlockSpec((d, tf), lambda i, j: (0, j)),
            pl.BlockSpec((tf, d), lambda i, j: (j, 0)),
        ],
        out_specs=pl.BlockSpec((tm, d), lambda i, j: (i, 0)),
        out_shape=jax.ShapeDtypeStruct((m, d), F32),
        scratch_shapes=[pltpu.VMEM((tm, d), BF16), pltpu.VMEM((tm, d), F32)],
        compiler_params=_params(("parallel", "arbitrary")),
        name="ffn_half",
    )(h, g_pre, g_post, w1, w3, w2)


def _proj_kernel(*refs, out_dtypes, scales):
    h_ref, g_ref = refs[0], refs[1]
    n_w = len(out_dtypes)
    w_refs = refs[2:2 + n_w]
    o_refs = iter(refs[2 + n_w:])
    x = _rms(h_ref[...], g_ref[...]).astype(BF16)
    for w_ref, dts, s in zip(w_refs, out_dtypes, scales):
        y = _dot(x, w_ref[...])
        if s != 1.0:
            y = y * s
        for dt in dts:
            next(o_refs)[...] = y.astype(dt)


def norm_proj(h, g, weights, out_dtypes, scales, *, tm):
    m, d = h.shape
    in_specs = [pl.BlockSpec((tm, d), lambda i: (i, 0)), pl.BlockSpec((1, d), lambda i: (0, 0))]
    in_specs += [pl.BlockSpec(w.shape, lambda i: (0, 0)) for w in weights]
    outs = [(w.shape[1], dt) for w, dts in zip(weights, out_dtypes) for dt in dts]
    return pl.pallas_call(
        functools.partial(_proj_kernel, out_dtypes=tuple(tuple(d) for d in out_dtypes), scales=tuple(scales)),
        grid=(m // tm,),
        in_specs=in_specs,
        out_specs=[pl.BlockSpec((tm, n), lambda i: (i, 0)) for n, _ in outs],
        out_shape=[jax.ShapeDtypeStruct((m, n), dt) for n, dt in outs],
        compiler_params=_params(("parallel",)),
        name="norm_proj",
    )(h, g, *weights)


def _outproj_kernel(*refs, n_in, tm, lead):
    h_ref, g_ref = refs[0], refs[1]
    y_refs = refs[2:2 + n_in]
    w_refs = refs[2 + n_in:2 + 2 * n_in]
    o_ref = refs[2 + 2 * n_in]
    m = _dot(y_refs[0][0], w_refs[0][...])
    for y_ref, w_ref in zip(y_refs[1:], w_refs[1:]):
        m = m + _dot(y_ref[0], w_ref[...])
    out = h_ref[0] + _rms(m, g_ref[...])
    pos = pl.program_id(1) * tm + lax.broadcasted_iota(jnp.int32, (tm, 1), 0)
    o_ref[0] = jnp.where(pos >= lead, out, 0.0)


def out_proj(h, g, ys, ws, *, tm, lead):
    b, lp, d = h.shape
    n = len(ys)
    in_specs = [pl.BlockSpec((1, tm, d), lambda i, j: (i, j, 0)), pl.BlockSpec((1, d), lambda i, j: (0, 0))]
    in_specs += [pl.BlockSpec((1, tm, y.shape[2]), lambda i, j: (i, j, 0)) for y in ys]
    in_specs += [pl.BlockSpec(w.shape, lambda i, j: (0, 0)) for w in ws]
    return pl.pallas_call(
        functools.partial(_outproj_kernel, n_in=n, tm=tm, lead=lead),
        grid=(b, lp // tm),
        in_specs=in_specs,
        out_specs=pl.BlockSpec((1, tm, d), lambda i, j: (i, j, 0)),
        out_shape=jax.ShapeDtypeStruct((b, lp, d), F32),
        compiler_params=_params(("parallel", "parallel")),
        name="out_proj",
    )(h, g, *ys, *ws)


def _sb_attn_kernel(q_ref, k_ref, v_ref, u2_ref, o_ref, *, tq, ts, lead):
    q0 = pl.program_id(2) * tq
    lane = lax.broadcasted_iota(jnp.int32, (1, LANES), 1)
    head0 = lane < HEAD_DIM
    rel = lax.broadcasted_iota(jnp.int32, (ts, ts), 1) - lax.broadcasted_iota(jnp.int32, (ts, ts), 0)
    kcol = lax.broadcasted_iota(jnp.int32, (ts, ts), 1)

    def sub_tile(st, _):
        r0 = pl.multiple_of(st * ts, ts)
        qs0 = q0 + r0
        diag = qs0 // ts
        q = q_ref[0, pl.ds(r0, ts), :]
        zq = jnp.zeros_like(q)
        q2 = jnp.concatenate([jnp.where(head0, q, zq), jnp.where(head0, zq, q)], axis=0)
        def step(ks, carry, nb, floor):
            masked = floor is not None
            edge = lambda kp: (rel < (qs0 - kp)) & ((kp + kcol) >= floor)
            acc, run = carry
            zs = [_dot_nt(q2, k_ref[0, pl.ds(_aligned(ks + g * ts, ts), min(2, nb - g) * ts), :])
                  for g in range(0, nb, 2)]
            a_blocks = [None] * nb
            for kk in reversed(range(nb)):
                zz = zs[kk // 2][:, (kk % 2) * ts:(kk % 2 + 1) * ts]
                ls = _log_sigmoid(zz)
                lr = ls - zz
                if masked:
                    vis = edge(ks + kk * ts)
                    vis2 = jnp.concatenate([vis, vis], axis=0)
                    lr = jnp.where(vis2, lr, 0.0)
                hi, lo = _split2(lr)
                s2 = _dot(jnp.concatenate([hi, lo], axis=1), u2_ref[...])
                a = jnp.exp(ls + s2[:, :ts] + run)
                if masked:
                    a = jnp.where(vis2, a, 0.0)
                a_blocks[kk] = a.astype(BF16)
                run = run + s2[:, ts:]
            for g in range(0, nb, 2):
                grp = a_blocks[g:g + 2]
                a2 = grp[0] if len(grp) == 1 else jnp.concatenate(grp, axis=1)
                acc = acc + _dot(a2, v_ref[0, pl.ds(_aligned(ks + g * ts, ts), len(grp) * ts), :])
            return acc, run

        zero = jnp.zeros((2 * ts, LANES), F32)
        blk = lambda kb: pl.multiple_of(kb * ts, ts)
        gsz = min(4, k_ref.shape[1] // ts)
        last = diag // gsz
        first_start = jnp.minimum(gsz * last, k_ref.shape[1] // ts - gsz)
        carry = step(blk(first_start), (zero, zero), gsz, jnp.maximum(lead, gsz * last * ts))
        carry = lax.fori_loop(1, last, lambda j, c: step(blk(gsz * (last - j)), c, gsz, None), carry)
        carry = _loop_if(last >= 1, lambda c: step(0, c, gsz, lead), carry)
        acc = carry[0]
        o_ref[0, pl.ds(r0, ts), :] = jnp.where(head0, acc[:ts], acc[ts:]).astype(o_ref.dtype)
        return 0

    lax.fori_loop(0, tq // ts, sub_tile, 0)


def _aligned(x, m):
    return x if isinstance(x, int) else pl.multiple_of(x, m)


def _loop_if(pred, f, carry):
    return lax.fori_loop(0, pred.astype(jnp.int32), lambda j, c: f(c), carry)


def _suffix_matrix(tk):
    j = jnp.arange(tk)[:, None]
    s = jnp.arange(tk)[None, :]
    u = jnp.concatenate([(j > s).astype(BF16), jnp.ones((tk, tk), BF16)], axis=1)
    return jnp.concatenate([u, u], axis=0)


def sb_attention(q, k, v, *, tq, lead):
    b, lp, dq = q.shape
    ts = ATTN_BLOCK
    u2 = _suffix_matrix(ts)
    return pl.pallas_call(
        functools.partial(_sb_attn_kernel, tq=tq, ts=ts, lead=lead),
        grid=(b, dq // LANES, lp // tq),
        in_specs=[
            pl.BlockSpec((1, tq, LANES), lambda i, p, j: (i, j, p)),
            pl.BlockSpec((1, lp, LANES), lambda i, p, j: (i, 0, p)),
            pl.BlockSpec((1, lp, LANES), lambda i, p, j: (i, 0, p)),
            pl.BlockSpec(u2.shape, lambda i, p, j: (0, 0)),
        ],
        out_specs=pl.BlockSpec((1, tq, LANES), lambda i, p, j: (i, j, p)),
        out_shape=jax.ShapeDtypeStruct((b, lp, dq), BF16),
        compiler_params=_params(("parallel", "parallel", "arbitrary")),
        name="sb_attention",
    )(q, k, v, u2)


def _fox_attn_kernel(q_ref, k_ref, v_ref, cq_ref, ckt_ref, o_ref, *, tq, ts, lead):
    pair = pl.program_id(1)
    q0 = pl.program_id(2) * tq
    lane = lax.broadcasted_iota(jnp.int32, (1, LANES), 1)
    head0 = lane < HEAD_DIM
    rel = lax.broadcasted_iota(jnp.int32, (ts, ts), 1) - lax.broadcasted_iota(jnp.int32, (ts, ts), 0)
    kcol = lax.broadcasted_iota(jnp.int32, (ts, ts), 1)

    def sub_tile(st, _):
        r0 = pl.multiple_of(st * ts, ts)
        qs0 = q0 + r0
        diag = qs0 // ts
        q = q_ref[0, pl.ds(r0, ts), :]
        cq_all = cq_ref[0, pl.ds(r0, ts), :]
        zq = jnp.zeros_like(q)
        q2 = jnp.concatenate([jnp.where(head0, q, zq), jnp.where(head0, zq, q)], axis=0)
        cq2 = jnp.concatenate([jnp.broadcast_to(
            jnp.sum(jnp.where(lane == pair * 2 + hh, cq_all, 0.0), axis=1, keepdims=True), (ts, ts))
            for hh in range(2)], axis=0)
        def scores(ks, nb, bias, lo):
            masked = lo is not None
            edge = lambda kp: (rel <= (qs0 - kp)) & ((kp + kcol) >= lo)
            zs =[_dot_nt(q2, k_ref[0, pl.ds(_aligned(ks + g * ts, ts), min(2, nb - g) * ts), :]) for g in range(0, nb, 2)]
            tiles = []
            for kk in range(nb):
                ck = jnp.concatenate(
                    [jnp.broadcast_to(ckt_ref[0, hh, :, pl.ds(_aligned(ks + kk * ts, ts), ts)], (ts, ts)) for hh in range(2)],
                    axis=0)
                t = zs[kk // 2][:, (kk % 2) * ts:(kk % 2 + 1) * ts] + bias - ck
                if masked:
                    vis = edge(ks + kk * ts)
                    t = jnp.where(jnp.concatenate([vis, vis], axis=0), t, NEG_INF)
                tiles.append(t)
            return tiles

        blk = lambda kb: pl.multiple_of(kb * ts, ts)
        gsz = min(4, k_ref.shape[1] // ts)
        last = diag // gsz
        last_start = jnp.minimum(gsz * last, k_ref.shape[1] // ts - gsz)

        def sweep(f, carry):
            carry = f(0, carry, gsz, lead)
            carry = lax.fori_loop(1, last, lambda g, c: f(blk(gsz * g), c, gsz, None), carry)
            return _loop_if(last >= 1, lambda c: f(blk(last_start), c, gsz, gsz * last * ts), carry)

        def row_max(ks, mx, nb, lo):
            for t in scores(ks, nb, cq2, lo):
                mx = jnp.maximum(mx, t)
            return mx

        mx = sweep(row_max, jnp.full((2 * ts, ts), NEG_INF, F32))
        m2 = jnp.broadcast_to(jnp.max(mx, axis=1, keepdims=True), (2 * ts, ts))
        cm2 = cq2 - m2

        def accumulate(ks, carry, nb, lo):
            acc, lsum = carry
            if lo is not None:
                ps = [jnp.exp(t - m2) for t in scores(ks, nb, cq2, lo)]
            else:
                ps = [jnp.exp(t) for t in scores(ks, nb, cm2, None)]
            for p in ps:
                lsum = lsum + p
            for g in range(0, nb, 2):
                grp = [p.astype(BF16) for p in ps[g:g + 2]]
                p2 = grp[0] if len(grp) == 1 else jnp.concatenate(grp, axis=1)
                acc = acc + _dot(p2, v_ref[0, pl.ds(_aligned(ks + g * ts, ts), len(grp) * ts), :])
            return acc, lsum

        zero = jnp.zeros((2 * ts, LANES), F32)
        acc, lsum = sweep(accumulate, (zero, zero))
        out = acc / jnp.sum(lsum, axis=1, keepdims=True)
        o_ref[0, pl.ds(r0, ts), :] = jnp.where(head0, out[:ts], out[ts:]).astype(o_ref.dtype)
        return 0

    lax.fori_loop(0, tq // ts, sub_tile, 0)


def fox_attention(q, k, v, cum, cum_t, *, tq, lead):
    b, lp, dq = q.shape
    return pl.pallas_call(
        functools.partial(_fox_attn_kernel, tq=tq, ts=ATTN_BLOCK, lead=lead),
        grid=(b, dq // LANES, lp // tq),
        in_specs=[
            pl.BlockSpec((1, tq, LANES), lambda i, p, j: (i, j, p)),
            pl.BlockSpec((1, lp, LANES), lambda i, p, j: (i, 0, p)),
            pl.BlockSpec((1, lp, LANES), lambda i, p, j: (i, 0, p)),
            pl.BlockSpec((1, tq, LANES), lambda i, p, j: (i, j, 0)),
            pl.BlockSpec((1, 2, 1, lp), lambda i, p, j: (i, p, 0, 0)),
        ],
        out_specs=pl.BlockSpec((1, tq, LANES), lambda i, p, j: (i, j, p)),
        out_shape=jax.ShapeDtypeStruct((b, lp, dq), BF16),
        compiler_params=_params(("parallel", "parallel", "arbitrary")),
        name="fox_attention",
    )(q, k, v, cum, cum_t)


def _logf_cum_kernel(fl_ref, bf_ref, logf_ref, cum_ref, cumt_ref, carry_ref, carryt_ref, *, lead):
    j = pl.program_id(1)

    @pl.when(j == 0)
    def _():
        carry_ref[...] = jnp.zeros_like(carry_ref)
        carryt_ref[...] = jnp.zeros_like(carryt_ref)

    x = fl_ref[0] + bf_ref[...]
    logf = jnp.minimum(x, 0.0) - jnp.log1p(jnp.exp(-jnp.abs(x)))
    logf_ref[0] = logf
    tb = logf.shape[0]
    pos = j * tb + lax.broadcasted_iota(jnp.int32, (tb, 1), 0)
    lz = jnp.where(pos >= lead, logf, 0.0)
    ri = lax.broadcasted_iota(jnp.int32, (tb, tb), 0)
    ci = lax.broadcasted_iota(jnp.int32, (tb, tb), 1)
    tril = (ci <= ri).astype(BF16)
    triu = (ri <= ci).astype(BF16)
    c = carry_ref[...]
    ct = carryt_ref[...]
    for p in _split3(lz):
        c = c + _dot(tril, p)
        ct = ct + _dot_tn(p, triu)
    cum_ref[0] = c
    cumt_ref[0] = ct
    carry_ref[...] = jnp.broadcast_to(c[tb - 1:tb, :], carry_ref.shape)
    carryt_ref[...] = jnp.broadcast_to(ct[:, tb - 1:tb], carryt_ref.shape)


def logf_cumsum(fl, b_f, *, lead):
    b, lp, _ = fl.shape
    tb = LANES
    return pl.pallas_call(
        functools.partial(_logf_cum_kernel, lead=lead),
        grid=(b, lp // tb),
        in_specs=[
            pl.BlockSpec((1, tb, LANES), lambda i, j: (i, j, 0)),
            pl.BlockSpec((1, LANES), lambda i, j: (0, 0)),
        ],
        out_specs=[
            pl.BlockSpec((1, tb, LANES), lambda i, j: (i, j, 0)),
            pl.BlockSpec((1, tb, LANES), lambda i, j: (i, j, 0)),
            pl.BlockSpec((1, LANES, tb), lambda i, j: (i, 0, j)),
        ],
        out_shape=[
            jax.ShapeDtypeStruct((b, lp, LANES), F32),
            jax.ShapeDtypeStruct((b, lp, LANES), F32),
            jax.ShapeDtypeStruct((b, LANES, lp), F32),
        ],
        scratch_shapes=[pltpu.VMEM((tb, LANES), F32), pltpu.VMEM((LANES, tb), F32)],
        compiler_params=_params(("parallel", "arbitrary")),
        name="logf_cumsum",
    )(fl, b_f)


def _pair_masks(rows):
    lane = lax.broadcasted_iota(jnp.int32, (rows, LANES), 1)
    return lane < HEAD_DIM, lane >= HEAD_DIM


def _bd(x):
    m0, m1 = _pair_masks(x.shape[0])
    zero = jnp.zeros_like(x)
    return jnp.concatenate([jnp.where(m0, x, zero), jnp.where(m1, x, zero)], axis=0)


def _rwkv_pair_chunk(r, k, v, av, b, ld, bds, sums=None):
    c = r.shape[0]
    cs, tot = _decay_sums(ld) if sums is None else sums
    e_pos = jnp.exp(cs)
    e_neg = jnp.exp(-cs)
    at = (av * jnp.exp(cs - ld)).astype(BF16)
    rt = (r * e_pos).astype(BF16)
    bt = b * e_neg
    kt = k * e_neg
    e_end = jnp.exp(cs[c - 1:c, :] - cs)
    bh = (b * e_end).astype(BF16)
    kh = (k * e_end).astype(BF16)

    lhs = jnp.concatenate([at, rt], axis=0)
    rhs = jnp.concatenate([_bd(bt), _bd(kt)], axis=0).astype(BF16)
    sc = _dot_nt(lhs, rhs)
    t2 = lax.broadcasted_iota(jnp.int32, (c, 2 * c), 0)
    s2 = lax.broadcasted_iota(jnp.int32, (c, 2 * c), 1) & (c - 1)
    strict = s2 < t2
    incl = s2 <= t2
    a_ab = jnp.where(strict, sc[:c, :2 * c], 0.0)
    a_ak = jnp.where(strict, sc[:c, 2 * c:], 0.0)
    a_rb = jnp.where(incl, sc[c:, :2 * c], 0.0)
    a_rk = jnp.where(incl, sc[c:, 2 * c:], 0.0)

    inv = jnp.where(s2 == t2, 1.0, 0.0) + a_ab
    lp = a_ab
    n_sq = max(c.bit_length() - 2, 0)
    for _ in range(n_sq):
        lp = _dot(lp.astype(BF16), _bd(lp).astype(BF16))
        inv = inv + _dot(inv.astype(BF16), _bd(lp).astype(BF16))

    bds16 = bds.astype(BF16)
    vbd = _bd(v).astype(BF16)
    rhs_u = _dot(at, bds16) + _dot(a_ak.astype(BF16), vbd)
    u = _dot(inv.astype(BF16), _bd(rhs_u).astype(BF16))
    ubd = _bd(u).astype(BF16)
    y = (_dot(rt, bds16)
         + _dot(jnp.concatenate([a_rb, a_rk], axis=1).astype(BF16), jnp.concatenate([ubd, vbd], axis=0)))
    cross = _dot_tn(bh, u.astype(BF16)) + _dot_tn(kh, v.astype(BF16))
    bds_new = jnp.exp(tot) * bds + jnp.where(_same_head(LANES, LANES), cross, 0.0)
    return y, bds_new


def _same_head(rows, cols):
    shift = HEAD_DIM.bit_length() - 1
    rr = lax.broadcasted_iota(jnp.int32, (rows, cols), 0) >> shift
    cc = lax.broadcasted_iota(jnp.int32, (rows, cols), 1) >> shift
    return rr == cc


def _head_sum(x, scale=1.0):
    w = x.shape[1]
    g = jnp.where(_same_head(w, w), scale, 0.0).astype(BF16)
    hi, lo = _split2(x)
    return _dot(hi, g) + _dot(lo, g)


def _decay_sums(ld):
    c, w = ld.shape
    tril = (lax.broadcasted_iota(jnp.int32, (c, c), 1) <= lax.broadcasted_iota(jnp.int32, (c, c), 0)).astype(BF16)
    ones = jnp.ones((c, LANES), BF16)
    cs = jnp.zeros((c, w), F32)
    tot = jnp.zeros((w, LANES), F32)
    for part in _split3(ld):
        cs = cs + _dot(tril, part)
        tot = tot + _dot_tn(part, ones)
    return cs, tot


def _softplus(x):
    return jnp.maximum(x, 0.0) + jnp.log1p(jnp.exp(-jnp.abs(x)))


def _rwkv_prep(ps, w0, w2p, a0, a2p, g2):
    da = w0.shape[1]
    x12 = ps[:, 3 * da:3 * da + LANES]
    lg = ps[:, 3 * da + LANES:3 * da + 2 * LANES]
    w_log = -_softplus(-(w0 + _dot(jnp.tanh(x12).astype(BF16), w2p))) - 0.5
    ld = -jnp.exp(w_log)
    a = jax.nn.sigmoid(a0 + _dot(x12.astype(BF16), a2p))
    g = _dot(jax.nn.sigmoid(lg).astype(BF16), g2)
    return ld, a, g


def _rwkv_pair_inputs(ps, a, kk_w, ka_w, pr):
    da = a.shape[1]
    sl = slice(LANES * pr, LANES * (pr + 1))
    r = ps[:, sl]
    k = ps[:, da + LANES * pr:da + LANES * (pr + 1)]
    v = ps[:, 2 * da + LANES * pr:2 * da + LANES * (pr + 1)]
    kk = k * kk_w[:, sl]
    kk = kk / jnp.maximum(jnp.sqrt(_head_sum(kk * kk)), 1e-12)
    ap = a[:, sl]
    k2 = k * (1.0 + (ap - 1.0) * ka_w[:, sl])
    return r, k2, v, kk, ap


def _rwkv_pair_output(y, r, k2, v, g, rk_w, lnw, lnb, pr):
    sl = slice(LANES * pr, LANES * (pr + 1))
    mean = _head_sum(y, 1.0 / HEAD_DIM)
    d = y - mean
    var = _head_sum(d * d, 1.0 / HEAD_DIM)
    yn = d * lax.rsqrt(var + GN_EPS) * lnw[:, sl] + lnb[:, sl]
    bonus = _head_sum(r * k2 * rk_w[:, sl]) * v
    return (yn + bonus) * g[:, sl]


def _rwkv_mix_kernel(p_ref, mu_ref, w0_ref, w2_ref, a0_ref, a2_ref, g2_ref, kk_ref, ka_ref, rk_ref,
                     lnw_ref, lnb_ref, ya_ref, sfin_ref, prev_ref, st_ref, *, c, n_pairs):
    j = pl.program_id(1)

    @pl.when(j == 0)
    def _():
        prev_ref[...] = jnp.zeros_like(prev_ref)
        st_ref[...] = jnp.zeros_like(st_ref)

    p = p_ref[0]
    row = lax.broadcasted_iota(jnp.int32, (c, 1), 0)
    p_prev = jnp.where(row == 0, prev_ref[0:1, :], pltpu.roll(p, 1, 0))
    prev_ref[...] = jnp.broadcast_to(p[c - 1:c, :], prev_ref.shape)
    ps = p + (p_prev - p) * mu_ref[...]
    ld, a, g = _rwkv_prep(ps, w0_ref[...], w2_ref[...], a0_ref[...], a2_ref[...], g2_ref[...])
    da = a.shape[1]
    r, k, v = ps[:, :da], ps[:, da:2 * da], ps[:, 2 * da:3 * da]
    kk = k * kk_ref[...]
    kk = kk / jnp.maximum(jnp.sqrt(_head_sum(kk * kk)), 1e-12)
    k2 = k * (1.0 + (a - 1.0) * ka_ref[...])
    bonus = _head_sum(r * k2 * rk_ref[...]) * v
    kka = kk * a
    cs, tot = _decay_sums(ld)
    ys = []
    for pr in range(n_pairs):
        sl = slice(LANES * pr, LANES * (pr + 1))
        y, bds = _rwkv_pair_chunk(r[:, sl], k2[:, sl], v[:, sl], -kk[:, sl], kka[:, sl], ld[:, sl], st_ref[pr],
                                  sums=(cs[:, sl], tot[LANES * pr:LANES * (pr + 1), :]))
        st_ref[pr] = bds
        ys.append(y)
    y = jnp.concatenate(ys, axis=1)
    d = y - _head_sum(y, 1.0 / HEAD_DIM)
    var = _head_sum(d * d, 1.0 / HEAD_DIM)
    yn = d * lax.rsqrt(var + GN_EPS) * lnw_ref[...] + lnb_ref[...]
    ya_ref[0] = ((yn + bonus) * g).astype(ya_ref.dtype)

    @pl.when(j == pl.num_programs(1) - 1)
    def _():
        sfin_ref[0] = st_ref[...]


def rwkv_mix_prompt(p, mu, w0, w2p, a0, a2p, g2, kk_w, ka_w, rk_w, lnw, lnb):
    b, lp, ds = p.shape
    da = w0.shape[1]
    n_pairs = da // LANES
    c = RWKV_CHUNK
    vec = lambda n: pl.BlockSpec((1, n), lambda i, j: (0, 0))
    mat = lambda w: pl.BlockSpec(w.shape, lambda i, j: (0, 0))
    return pl.pallas_call(
        functools.partial(_rwkv_mix_kernel, c=c, n_pairs=n_pairs),
        grid=(b, lp // c),
        in_specs=[pl.BlockSpec((1, c, ds), lambda i, j: (i, j, 0)), vec(ds), vec(da), mat(w2p), vec(da), mat(a2p),
                  mat(g2), vec(da), vec(da), vec(da), vec(da), vec(da)],
        out_specs=[pl.BlockSpec((1, c, da), lambda i, j: (i, j, 0)),
                   pl.BlockSpec((1, n_pairs, LANES, LANES), lambda i, j: (i, 0, 0, 0))],
        out_shape=[jax.ShapeDtypeStruct((b, lp, da), BF16),
                   jax.ShapeDtypeStruct((b, n_pairs, LANES, LANES), F32)],
        scratch_shapes=[pltpu.VMEM((8, ds), F32), pltpu.VMEM((n_pairs, LANES, LANES), F32)],
        compiler_params=_params(("parallel", "arbitrary")),
        name="rwkv_mix_prompt",
    )(p, mu, w0, w2p, a0, a2p, g2, kk_w, ka_w, rk_w, lnw, lnb)


def _rwkv_step_kernel(p_ref, prev_ref, s_ref, mu_ref, w0_ref, w2_ref, a0_ref, a2_ref, g2_ref, kk_ref, ka_ref,
                      rk_ref, lnw_ref, lnb_ref, ya_ref, snew_ref, r_s, w_s, k_s, v_s, a_s, b_s, y_s,
                      *, n_pairs):
    p = p_ref[...]
    ps = p + (prev_ref[...] - p) * mu_ref[...]
    ld, a, g = _rwkv_prep(ps, w0_ref[...], w2_ref[...], a0_ref[...], a2_ref[...], g2_ref[...])
    pair_vals = []
    for pr in range(n_pairs):
        sl = slice(LANES * pr, LANES * (pr + 1))
        r, k2, v, kk, ap = _rwkv_pair_inputs(ps, a, kk_ref[...], ka_ref[...], pr)
        pair_vals.append((r, k2, v))
        r_s[:, sl] = r
        w_s[:, sl] = jnp.exp(ld[:, sl])
        k_s[:, sl] = k2
        v_s[:, sl] = v
        a_s[:, sl] = -kk
        b_s[:, sl] = kk * ap

    hd = HEAD_DIM
    eye = lax.broadcasted_iota(jnp.int32, (hd, hd), 0) == lax.broadcasted_iota(jnp.int32, (hd, hd), 1)
    sel = (lax.broadcasted_iota(jnp.int32, (hd, LANES), 1) & (hd - 1)) == lax.broadcasted_iota(
        jnp.int32, (hd, LANES), 0)
    lane = lax.broadcasted_iota(jnp.int32, (1, LANES), 1)

    def body(bg, carry):
        b0 = pl.multiple_of(bg * 8, 8)
        for pr in range(n_pairs):
            sl = slice(LANES * pr, LANES * (pr + 1))
            tiles = {}
            for name, ref in (("a", a_s), ("b", b_s), ("w", w_s), ("k", k_s), ("r", r_s), ("v", v_s)):
                x = ref[pl.ds(b0, 8), sl]
                tiles[name] = (x, pltpu.roll(x, hd, 1))
            y_rows = []
            for i in range(8):
                heads = []
                for hh in range(2):
                    vec = lambda name: tiles[name][hh][i:i + 1, :hd]
                    s = s_ref[b0 + i, 2 * pr + hh]
                    sa = jnp.sum(s * vec("a"), axis=1, keepdims=True)
                    v_col = jnp.sum(jnp.where(eye, vec("v"), 0.0), axis=1, keepdims=True)
                    s2 = s * vec("w") + sa * vec("b") + v_col * vec("k")
                    snew_ref[b0 + i, 2 * pr + hh] = s2
                    y_col = jnp.sum(s2 * vec("r"), axis=1, keepdims=True)
                    heads.append(jnp.sum(jnp.where(sel, y_col, 0.0), axis=0, keepdims=True))
                y_rows.append(jnp.where(lane < hd, heads[0], heads[1]))
            y_s[pl.ds(b0, 8), sl] = jnp.concatenate(y_rows, axis=0)
        return carry

    lax.fori_loop(0, p.shape[0] // 8, body, 0)
    y = y_s[...]
    for pr in range(n_pairs):
        sl = slice(LANES * pr, LANES * (pr + 1))
        r, k2, v = pair_vals[pr]
        out = _rwkv_pair_output(y[:, sl], r, k2, v, g, rk_ref[...], lnw_ref[...], lnb_ref[...], pr)
        ya_ref[:, sl] = out.astype(ya_ref.dtype)


def rwkv_step(p, prev, state, mu, w0, w2p, a0, a2p, g2, kk_w, ka_w, rk_w, lnw, lnb):
    db, ds = p.shape
    da = w0.shape[1]
    return pl.pallas_call(
        functools.partial(_rwkv_step_kernel, n_pairs=da // LANES),
        out_shape=[jax.ShapeDtypeStruct((db, da), BF16), jax.ShapeDtypeStruct(state.shape, F32)],
        scratch_shapes=[pltpu.VMEM((db, da), F32) for _ in range(7)],
        compiler_params=pltpu.CompilerParams(vmem_limit_bytes=VMEM_LIMIT),
        name="rwkv_step",
    )(p, prev, state, mu, w0, w2p, a0, a2p, g2, kk_w, ka_w, rk_w, lnw, lnb)


def _head_rows(q_row, rows):
    width = q_row.shape[1]
    row = lax.broadcasted_iota(jnp.int32, (rows, width), 0)
    col = lax.broadcasted_iota(jnp.int32, (rows, width), 1) >> 6
    return jnp.where(row == col, q_row, 0.0)


def _diag_blocks(acc):
    row = lax.broadcasted_iota(jnp.int32, acc.shape, 0)
    col = lax.broadcasted_iota(jnp.int32, acc.shape, 1) >> 6
    return jnp.sum(jnp.where(row == col, acc, 0.0), axis=0, keepdims=True)


def _log_sigmoid(x):
    return jnp.minimum(x, 0.0) - jnp.log1p(jnp.exp(-jnp.abs(x)))


def _sb_decode_kernel(pt_ref, q_ref, kn_ref, vn_ref, u2_ref, *rest, pp, n_pages, n_heads):
    k_refs, v_refs = rest[:pp], rest[pp:2 * pp]
    o_ref, qt_ref, acc_ref, run_ref = rest[2 * pp:]
    j = pl.program_id(1)
    qpos = n_pages * PAGE_SIZE
    lane = lax.broadcasted_iota(jnp.int32, (1, PAGE_SIZE), 1)

    def page(kpg, vpg, kpos0, n_valid):
        z = _dot(qt_ref[...], kpg.astype(BF16))[:n_heads]
        vis = ((kpos0 + lane) < qpos) & (lane < n_valid)
        ls = _log_sigmoid(z)
        lr = jnp.where(vis, ls - z, 0.0)
        hi, lo = _split2(lr)
        s2 = _dot(jnp.concatenate([hi, lo], axis=1), u2_ref[...])
        a = jnp.where(vis, jnp.exp(ls + s2[:, :PAGE_SIZE] + run_ref[...]), 0.0)
        acc_ref[...] += _dot_nt(a.astype(BF16), vpg.astype(BF16))
        run_ref[...] += s2[:, PAGE_SIZE:]

    @pl.when(j == 0)
    def _():
        qt_ref[...] = _head_rows(q_ref[0].astype(F32), qt_ref.shape[0]).astype(BF16)
        acc_ref[...] = jnp.zeros_like(acc_ref)
        run_ref[...] = jnp.zeros_like(run_ref)
        page(kn_ref[0], vn_ref[0], qpos, 1)

    for s in range(pp):
        page(k_refs[s][0, 0], v_refs[s][0, 0], (n_pages - 1 - (j * pp + s)) * PAGE_SIZE, PAGE_SIZE)

    @pl.when(j == pl.num_programs(1) - 1)
    def _():
        o_ref[0] = _diag_blocks(acc_ref[...]).astype(o_ref.dtype)


def _page_specs(layer, pp, n_pages, width):
    def spec(s):
        return pl.BlockSpec((1, 1, width, PAGE_SIZE),
                            lambda b, j, pt: (layer, pt[b, n_pages - 1 - (j * pp + s)], 0, 0))
    return [spec(s) for s in range(pp)]


def _cache_view(cache):
    nl, n_pool, rows, n_heads, hd = cache.shape
    return jnp.transpose(cache, (0, 1, 3, 4, 2)).reshape(nl, n_pool, n_heads * hd, rows)


def _new_token_page(x):
    return jnp.pad(x[:, :, None], ((0, 0), (0, 0), (0, PAGE_SIZE - 1)))


def sb_decode(q, k_new, v_new, cache_k, cache_v, layer, page_table, *, pp):
    db, _, d = q.shape
    n_pages = page_table.shape[1]
    n_heads = d // HEAD_DIM
    u2 = _suffix_matrix(PAGE_SIZE)
    row = lambda r, w: pl.BlockSpec((1, r, w), lambda b, j, pt: (b, 0, 0))
    grid_spec = pltpu.PrefetchScalarGridSpec(
        num_scalar_prefetch=1,
        grid=(db, n_pages // pp),
        in_specs=[row(1, d), row(d, PAGE_SIZE), row(d, PAGE_SIZE),
                  pl.BlockSpec(u2.shape, lambda b, j, pt: (0, 0))]
                 + _page_specs(layer, pp, n_pages, d) + _page_specs(layer, pp, n_pages, d),
        out_specs=row(1, d),
        scratch_shapes=[pltpu.VMEM((16, d), BF16), pltpu.VMEM((n_heads, d), F32),
                        pltpu.VMEM((n_heads, PAGE_SIZE), F32)],
    )
    return pl.pallas_call(
        functools.partial(_sb_decode_kernel, pp=pp, n_pages=n_pages, n_heads=n_heads),
        grid_spec=grid_spec,
        out_shape=jax.ShapeDtypeStruct((db, 1, d), BF16),
        compiler_params=_params(("parallel", "arbitrary")),
        name="sb_decode",
    )(page_table, q, k_new, v_new, u2, *([cache_k] * pp), *([cache_v] * pp))


def _fox_decode_kernel(pt_ref, q_ref, kn_ref, vn_ref, fl_ref, bf_ref, m2_ref, *rest, pp, n_pages, n_heads):
    k_refs, v_refs, lf_refs = rest[:pp], rest[pp:2 * pp], rest[2 * pp:3 * pp]
    o_ref, lfo_ref, qt_ref, acc_ref, m_ref, l_ref, run_ref = rest[3 * pp:]
    j = pl.program_id(1)
    qpos = n_pages * PAGE_SIZE
    lane = lax.broadcasted_iota(jnp.int32, (1, PAGE_SIZE), 1)

    def page(kpg, vpg, bias, kpos0, n_valid):
        z = _dot(qt_ref[...], kpg.astype(BF16))[:n_heads] + bias
        vis = ((kpos0 + lane) <= qpos) & (lane < n_valid)
        z = jnp.where(vis, z, NEG_INF)
        m_new = jnp.maximum(m_ref[...], jnp.max(z, axis=1, keepdims=True))
        alpha = jnp.exp(m_ref[...] - m_new)
        p = jnp.exp(z - m_new)
        l_ref[...] = alpha * l_ref[...] + jnp.sum(p, axis=1, keepdims=True)
        acc_ref[...] = alpha * acc_ref[...] + _dot_nt(p.astype(BF16), vpg.astype(BF16))
        m_ref[...] = m_new

    @pl.when(j == 0)
    def _():
        qt_ref[...] = _head_rows(q_ref[0].astype(F32), qt_ref.shape[0]).astype(BF16)
        acc_ref[...] = jnp.zeros_like(acc_ref)
        m_ref[...] = jnp.full_like(m_ref, NEG_INF)
        l_ref[...] = jnp.zeros_like(l_ref)
        logf = _log_sigmoid(fl_ref[0] + bf_ref[...])
        lfo_ref[0] = logf
        eye = lax.broadcasted_iota(jnp.int32, (LANES, LANES), 0) == lax.broadcasted_iota(
            jnp.int32, (LANES, LANES), 1)
        col = jnp.sum(jnp.where(eye, logf, 0.0), axis=1, keepdims=True)
        run_ref[...] = jnp.broadcast_to(col[:n_heads], run_ref.shape)
        page(kn_ref[0], vn_ref[0], 0.0, qpos, 1)

    for s in range(pp):
        lf = lf_refs[s][0]
        s2 = jnp.zeros((n_heads, 2 * PAGE_SIZE), F32)
        for part in _split3(lf):
            s2 = s2 + _dot(part, m2_ref[...])
        page(k_refs[s][0, 0], v_refs[s][0, 0], run_ref[...] + s2[:, :PAGE_SIZE],
             (n_pages - 1 - (j * pp + s)) * PAGE_SIZE, PAGE_SIZE)
        run_ref[...] += s2[:, PAGE_SIZE:]

    @pl.when(j == pl.num_programs(1) - 1)
    def _():
        o_ref[0] = _diag_blocks(acc_ref[...] / l_ref[...]).astype(o_ref.dtype)


def fox_decode(q, k_new, v_new, fl_new, b_f, cache_k, cache_v, layer, cache_lf_t, page_table, *, pp):
    db, _, d = q.shape
    n_pages = page_table.shape[1]
    n_heads = d // HEAD_DIM
    jj = jnp.arange(PAGE_SIZE)[:, None]
    ss = jnp.arange(PAGE_SIZE)[None, :]
    m2 = jnp.concatenate([(jj > ss).astype(BF16), jnp.ones((PAGE_SIZE, PAGE_SIZE), BF16)], axis=1)
    row = lambda r, w: pl.BlockSpec((1, r, w), lambda b, j, pt: (b, 0, 0))
    lf_specs = [pl.BlockSpec((1, n_heads, PAGE_SIZE),
                             lambda b, j, pt, s=s: (pt[b, n_pages - 1 - (j * pp + s)], 0, 0)) for s in range(pp)]
    grid_spec = pltpu.PrefetchScalarGridSpec(
        num_scalar_prefetch=1,
        grid=(db, n_pages // pp),
        in_specs=[row(1, d), row(d, PAGE_SIZE), row(d, PAGE_SIZE), row(1, LANES),
                  pl.BlockSpec((1, LANES), lambda b, j, pt: (0, 0)),
                  pl.BlockSpec(m2.shape, lambda b, j, pt: (0, 0))]
                 + _page_specs(layer, pp, n_pages, d) + _page_specs(layer, pp, n_pages, d) + lf_specs,
        out_specs=[row(1, d), row(1, LANES)],
        scratch_shapes=[pltpu.VMEM((16, d), BF16), pltpu.VMEM((n_heads, d), F32),
                        pltpu.VMEM((n_heads, 1), F32), pltpu.VMEM((n_heads, 1), F32),
                        pltpu.VMEM((n_heads, PAGE_SIZE), F32)],
    )
    return pl.pallas_call(
        functools.partial(_fox_decode_kernel, pp=pp, n_pages=n_pages, n_heads=n_heads),
        grid_spec=grid_spec,
        out_shape=[jax.ShapeDtypeStruct((db, 1, d), BF16), jax.ShapeDtypeStruct((db, 1, LANES), F32)],
        compiler_params=_params(("parallel", "arbitrary")),
        name="fox_decode",
    )(page_table, q, k_new, v_new, fl_new, b_f, m2, *([cache_k] * pp), *([cache_v] * pp), *([cache_lf_t] * pp))


def _pad_lanes(x):
    return jnp.pad(x, [(0, 0)] * (x.ndim - 1) + [(0, LANES - x.shape[-1])])


def kernel(x_prompt, x_sample, cache_sb_k, cache_sb_v, cache_fox_k, cache_fox_v, cache_fox_logf, state_rwkv, state_rwkv_shift, page_table, meta_tokens, norm_g, ffn_w1, ffn_w3, ffn_w2, w_in_even, w_out_even, rwkv_mu, rwkv_w0, rwkv_w2, rwkv_a0, rwkv_a2, rwkv_g2, rwkv_kk, rwkv_ka, rwkv_rk, rwkv_lnw, rwkv_lnb, w_in_odd, b_f, w_out_odd):
    bsz, seq, d = x_prompt.shape
    db, ts, _ = x_sample.shape
    assert ts == 1, "the decode kernels handle one new token per sequence"
    depth = norm_g.shape[0]
    t_p = N_META + seq
    lp = LEAD + t_p
    n_pool, n_pages = cache_sb_k.shape[1], page_table.shape[1]
    h_a = state_rwkv.shape[2]
    d_a = h_a * HEAD_DIM
    d_b = cache_sb_k.shape[3] * HEAD_DIM
    d_c = cache_fox_k.shape[3] * HEAD_DIM
    h_c = d_c // HEAD_DIM
    d_shift = state_rwkv_shift.shape[2]
    dff = ffn_w1.shape[3]

    hp = jnp.concatenate([jnp.zeros((bsz, LEAD, d), F32),
                          jnp.broadcast_to(meta_tokens[None].astype(F32), (bsz, N_META, d)), x_prompt], axis=1)
    hs = x_sample.reshape(db, d)

    tm = _row_tile(lp)
    tq = tm
    tf = dff // 2 if (dff // 2) % LANES == 0 else dff
    pp = min(8, n_pages)
    gvec = lambda l, i: norm_g[l, i][None, :]

    def ffn(h2, l, i):
        w1, w3, w2 = (w[l, i].astype(BF16) for w in (ffn_w1, ffn_w3, ffn_w2))
        return ffn_half(h2, gvec(l, 2 * i * 2), gvec(l, 2 * i * 2 + 1), w1, w3, w2,
                        tm=_row_tile(h2.shape[0]), tf=tf)

    outs_p = {k: [] for k in ("sb_k", "sb_v", "fk", "fv", "fl", "rw", "sh")}
    outs_s = {k: [] for k in ("sb_k", "sb_v", "fk", "fv", "fl", "rw", "sh")}

    for l in range(depth):
        hp = ffn(hp.reshape(bsz * lp, d), l, 0).reshape(bsz, lp, d)
        hs = ffn(hs, l, 0)
        g_mix, g_post = gvec(l, 2), gvec(l, 3)
        if l % 2 == 0:
            e = l // 2
            w_in = w_in_even[e].astype(BF16)
            slabs = [w_in[:, :d_shift], w_in[:, d_shift:d_shift + d_b],
                     w_in[:, d_shift + d_b:d_shift + 2 * d_b], w_in[:, d_shift + 2 * d_b:]]
            dts = [(F32,), (BF16,), (F32, BF16), (F32, BF16)]
            scl = [1.0, HEAD_DIM ** -0.5, 1.0, 1.0]
            zeros_l = jnp.zeros((LORA_W, d_a), F32)
            rw = (rwkv_mu[e][None], rwkv_w0[e][None],
                  jnp.concatenate([rwkv_w2[e], zeros_l], axis=0).astype(BF16), rwkv_a0[e][None],
                  jnp.concatenate([zeros_l, rwkv_a2[e]], axis=0).astype(BF16), rwkv_g2[e].astype(BF16),
                  rwkv_kk[e][None], rwkv_ka[e][None], rwkv_rk[e].reshape(1, d_a), rwkv_lnw[e][None],
                  rwkv_lnb[e][None])
            w_out = w_out_even[e].astype(BF16)
            p, q, k, kb, v, vb = norm_proj(hp.reshape(bsz * lp, d), g_mix, slabs, dts, scl, tm=tm)
            r3 = lambda x: x.reshape(bsz, lp, x.shape[-1])
            ya, sfin = rwkv_mix_prompt(r3(p), *rw)
            yb = sb_attention(r3(q), r3(kb), r3(vb), tq=tq, lead=LEAD)
            hp = out_proj(hp, g_post, [ya, yb], [w_out[:d_a], w_out[d_a:]], tm=tm, lead=LEAD)
            outs_p["sb_k"].append(r3(k)[:, LEAD:].reshape(bsz, t_p, -1, HEAD_DIM))
            outs_p["sb_v"].append(r3(v)[:, LEAD:].reshape(bsz, t_p, -1, HEAD_DIM))
            s_pairs = sfin.reshape(bsz, d_a // LANES, 2, HEAD_DIM, 2, HEAD_DIM)
            s_heads = jnp.stack([s_pairs[:, :, 0, :, 0, :], s_pairs[:, :, 1, :, 1, :]], axis=2)
            outs_p["rw"].append(jnp.swapaxes(s_heads.reshape(bsz, h_a, HEAD_DIM, HEAD_DIM), -1, -2))
            outs_p["sh"].append(r3(p)[:, -1])
            p, q, k, v = norm_proj(hs, g_mix, slabs, [(F32,), (BF16,), (F32,), (F32,)], scl, tm=_row_tile(db))
            ya, s_new = rwkv_step(p, state_rwkv_shift[e], state_rwkv[e], *rw)
            yb = sb_decode(q[:, None], _new_token_page(k), _new_token_page(v), _cache_view(cache_sb_k),
                           _cache_view(cache_sb_v), e, page_table, pp=pp)
            hs = out_proj(hs[None], g_post, [ya[None], yb.reshape(1, db, d_b)], [w_out[:d_a], w_out[d_a:]],
                          tm=_row_tile(db), lead=0)[0]
            outs_s["sb_k"].append(k.reshape(db, 1, -1, HEAD_DIM))
            outs_s["sb_v"].append(v.reshape(db, 1, -1, HEAD_DIM))
            outs_s["rw"].append(s_new)
            outs_s["sh"].append(p)
        else:
            o = l // 2
            w_in = w_in_odd[o]
            slabs = [w_in[:, :d_c].astype(BF16), w_in[:, d_c:2 * d_c].astype(BF16),
                     w_in[:, 2 * d_c:3 * d_c].astype(BF16), _pad_lanes(w_in[:, 3 * d_c:]).astype(BF16)]
            dts = [(BF16,), (F32, BF16), (F32, BF16), (F32,)]
            scl = [HEAD_DIM ** -0.5, 1.0, 1.0, 1.0]
            bf_row = _pad_lanes(b_f[o][None])
            w_out = w_out_odd[o].astype(BF16)
            q, k, kb, v, vb, fl = norm_proj(hp.reshape(bsz * lp, d), g_mix, slabs, dts, scl, tm=tm)
            r3 = lambda x: x.reshape(bsz, lp, x.shape[-1])
            logf, cum, cum_t = logf_cumsum(r3(fl), bf_row, lead=LEAD)
            yc = fox_attention(r3(q), r3(kb), r3(vb), cum, cum_t[:, :h_c, None, :], tq=tq, lead=LEAD)
            hp = out_proj(hp, g_post, [yc], [w_out], tm=tm, lead=LEAD)
            outs_p["fk"].append(r3(k)[:, LEAD:].reshape(bsz, t_p, h_c, HEAD_DIM))
            outs_p["fv"].append(r3(v)[:, LEAD:].reshape(bsz, t_p, h_c, HEAD_DIM))
            outs_p["fl"].append(logf[:, LEAD:, :h_c])
            q, k, v, fl = norm_proj(hs, g_mix, slabs, [(BF16,), (F32,), (F32,), (F32,)], scl, tm=_row_tile(db))
            yc, logf_s = fox_decode(q[:, None], _new_token_page(k), _new_token_page(v), fl[:, None], bf_row,
                                    _cache_view(cache_fox_k), _cache_view(cache_fox_v), o,
                                    jnp.swapaxes(cache_fox_logf[o], 1, 2), page_table, pp=pp)
            hs = out_proj(hs[None], g_post, [yc.reshape(1, db, d_c)], [w_out], tm=_row_tile(db), lead=0)[0]
            outs_s["fk"].append(k.reshape(db, 1, h_c, HEAD_DIM))
            outs_s["fv"].append(v.reshape(db, 1, h_c, HEAD_DIM))
            outs_s["fl"].append(logf_s[:, :, :h_c])
        hp = ffn(hp.reshape(bsz * lp, d), l, 1).reshape(bsz, lp, d)
        hs = ffn(hs, l, 1)

    y_prompt = hp[:, LEAD + N_META:]
    y_sample = hs.reshape(db, 1, d)
    order = ("sb_k", "sb_v", "fk", "fv", "fl", "rw", "sh")
    return (y_prompt, y_sample) + tuple(jnp.stack(outs_p[k]) for k in order) + tuple(
        jnp.stack(outs_s[k]) for k in order)
```

```python
import functools

import jax
import jax.numpy as jnp
from jax import lax
from jax.experimental import pallas as pl
from jax.experimental.pallas import tpu as pltpu

F32 = jnp.float32
BF16 = jnp.bfloat16

HEAD_DIM = 64
N_META = 16
ATTN_BLOCK = 128
LEAD = (-N_META) % ATTN_BLOCK
PAGE_SIZE = 128
LORA_W = 64
LORA_A = 64
LORA_G = 128
FFN_RES = 0.5
NORM_EPS = 1e-6
GN_EPS = 64e-5
NEG_INF = -1e30
LANES = 128
RWKV_CHUNK = 64
VMEM_LIMIT = 52 * 1024 * 1024


def _params(sem):
    return pltpu.CompilerParams(dimension_semantics=sem, vmem_limit_bytes=VMEM_LIMIT)


def _row_tile(m, candidates=(384, 256, 128, 64, 32, 16, 8)):
    for t in candidates:
        if m % t == 0:
            return t
    raise ValueError(f"no row tile for {m}")


def _rms(x, g):
    return x * lax.rsqrt(jnp.mean(x * x, axis=-1, keepdims=True) + NORM_EPS) * g


def _split3(x):
    h = x.astype(BF16)
    r = x - h.astype(F32)
    m = r.astype(BF16)
    l = (r - m.astype(F32)).astype(BF16)
    return h, m, l


def _split2(x):
    h = x.astype(BF16)
    return h, (x - h.astype(F32)).astype(BF16)


def _dot(a, b):
    return jnp.dot(a, b, preferred_element_type=F32)


def _dot_nt(a, b):
    return lax.dot_general(a, b, (((1,), (1,)), ((), ())), preferred_element_type=F32)


def _dot_tn(a, b):
    return lax.dot_general(a, b, (((0,), (0,)), ((), ())), preferred_element_type=F32)


def _ffn_kernel(h_ref, gpre_ref, gpost_ref, w1_ref, w3_ref, w2_ref, o_ref, xn_ref, acc_ref):
    j = pl.program_id(1)

    @pl.when(j == 0)
    def _():
        xn_ref[...] = _rms(h_ref[...], gpre_ref[...]).astype(BF16)
        acc_ref[...] = jnp.zeros_like(acc_ref)

    x = xn_ref[...]
    a = _dot(x, w1_ref[...])
    b = _dot(x, w3_ref[...])
    t = (a * jax.nn.sigmoid(a) * b).astype(BF16)
    acc_ref[...] += _dot(t, w2_ref[...])

    @pl.when(j == pl.num_programs(1) - 1)
    def _():
        o_ref[...] = h_ref[...] + FFN_RES * _rms(acc_ref[...], gpost_ref[...])


def ffn_half(h, g_pre, g_post, w1, w3, w2, *, tm, tf):
    m, d = h.shape
    dff = w1.shape[1]
    return pl.pallas_call(
        _ffn_kernel,
        grid=(m // tm, dff // tf),
        in_specs=[
            pl.BlockSpec((tm, d), lambda i, j: (i, 0)),
            pl.BlockSpec((1, d), lambda i, j: (0, 0)),
            pl.BlockSpec((1, d), lambda i, j: (0, 0)),
            pl.BlockSpec((d, tf), lambda i, j: (0, j)),
            pl.BlockSpec((d, tf), lambda i, j: (0, j)),
            pl.BlockSpec((tf, d), lambda i, j: (j, 0)),
        ],
        out_specs=pl.BlockSpec((tm, d), lambda i, j: (i, 0)),
        out_shape=jax.ShapeDtypeStruct((m, d), F32),
        scratch_shapes=[pltpu.VMEM((tm, d), BF16), pltpu.VMEM((tm, d), F32)],
        compiler_params=_params(("parallel", "arbitrary")),
        name="ffn_half",
    )(h, g_pre, g_post, w1, w3, w2)


def _proj_kernel(*refs, out_dtypes, scales):
    h_ref, g_ref = refs[0], refs[1]
    n_w = len(out_dtypes)
    w_refs = refs[2:2 + n_w]
    o_refs = iter(refs[2 + n_w:])
    x = _rms(h_ref[...], g_ref[...]).astype(BF16)
    for w_ref, dts, s in zip(w_refs, out_dtypes, scales):
        y = _dot(x, w_ref[...])
        if s != 1.0:
            y = y * s
        for dt in dts:
            next(o_refs)[...] = y.astype(dt)


def norm_proj(h, g, weights, out_dtypes, scales, *, tm):
    m, d = h.shape
    in_specs = [pl.BlockSpec((tm, d), lambda i: (i, 0)), pl.BlockSpec((1, d), lambda i: (0, 0))]
    in_specs += [pl.BlockSpec(w.shape, lambda i: (0, 0)) for w in weights]
    outs = [(w.shape[1], dt) for w, dts in zip(weights, out_dtypes) for dt in dts]
    return pl.pallas_call(
        functools.partial(_proj_kernel, out_dtypes=tuple(tuple(d) for d in out_dtypes), scales=tuple(scales)),
        grid=(m // tm,),
        in_specs=in_specs,
        out_specs=[pl.BlockSpec((tm, n), lambda i: (i, 0)) for n, _ in outs],
        out_shape=[jax.ShapeDtypeStruct((m, n), dt) for n, dt in outs],
        compiler_params=_params(("parallel",)),
        name="norm_proj",
    )(h, g, *weights)


def _outproj_kernel(*refs, n_in, tm, lead):
    h_ref, g_ref = refs[0], refs[1]
    y_refs = refs[2:2 + n_in]
    w_refs = refs[2 + n_in:2 + 2 * n_in]
    o_ref = refs[2 + 2 * n_in]
    m = _dot(y_refs[0][0], w_refs[0][...])
    for y_ref, w_ref in zip(y_refs[1:], w_refs[1:]):
        m = m + _dot(y_ref[0], w_ref[...])
    out = h_ref[0] + _rms(m, g_ref[...])
    pos = pl.program_id(1) * tm + lax.broadcasted_iota(jnp.int32, (tm, 1), 0)
    o_ref[0] = jnp.where(pos >= lead, out, 0.0)


def out_proj(h, g, ys, ws, *, tm, lead):
    b, lp, d = h.shape
    n = len(ys)
    in_specs = [pl.BlockSpec((1, tm, d), lambda i, j: (i, j, 0)), pl.BlockSpec((1, d), lambda i, j: (0, 0))]
    in_specs += [pl.BlockSpec((1, tm, y.shape[2]), lambda i, j: (i, j, 0)) for y in ys]
    in_specs += [pl.BlockSpec(w.shape, lambda i, j: (0, 0)) for w in ws]
    return pl.pallas_call(
        functools.partial(_outproj_kernel, n_in=n, tm=tm, lead=lead),
        grid=(b, lp // tm),
        in_specs=in_specs,
        out_specs=pl.BlockSpec((1, tm, d), lambda i, j: (i, j, 0)),
        out_shape=jax.ShapeDtypeStruct((b, lp, d), F32),
        compiler_params=_params(("parallel", "parallel")),
        name="out_proj",
    )(h, g, *ys, *ws)


def _sb_attn_kernel(q_ref, k_ref, v_ref, u2_ref, o_ref, *, tq, ts, lead):
    q0 = pl.program_id(2) * tq
    lane = lax.broadcasted_iota(jnp.int32, (1, LANES), 1)
    head0 = lane < HEAD_DIM
    rel = lax.broadcasted_iota(jnp.int32, (ts, ts), 1) - lax.broadcasted_iota(jnp.int32, (ts, ts), 0)
    kcol = lax.broadcasted_iota(jnp.int32, (ts, ts), 1)

    def sub_tile(st, _):
        r0 = pl.multiple_of(st * ts, ts)
        qs0 = q0 + r0
        diag = qs0 // ts
        q = q_ref[0, pl.ds(r0, ts), :]
        zq = jnp.zeros_like(q)
        q2 = jnp.concatenate([jnp.where(head0, q, zq), jnp.where(head0, zq, q)], axis=0)
        def step(ks, carry, nb, floor):
            masked = floor is not None
            edge = lambda kp: (rel < (qs0 - kp)) & ((kp + kcol) >= floor)
            acc, run = carry
            zs = [_dot_nt(q2, k_ref[0, pl.ds(_aligned(ks + g * ts, ts), min(2, nb - g) * ts), :])
                  for g in range(0, nb, 2)]
            a_blocks = [None] * nb
            for kk in reversed(range(nb)):
                zz = zs[kk // 2][:, (kk % 2) * ts:(kk % 2 + 1) * ts]
                ls = _log_sigmoid(zz)
                lr = ls - zz
                if masked:
                    vis = edge(ks + kk * ts)
                    vis2 = jnp.concatenate([vis, vis], axis=0)
                    lr = jnp.where(vis2, lr, 0.0)
                hi, lo = _split2(lr)
                s2 = _dot(jnp.concatenate([hi, lo], axis=1), u2_ref[...])
                a = jnp.exp(ls + s2[:, :ts] + run)
                if masked:
                    a = jnp.where(vis2, a, 0.0)
                a_blocks[kk] = a.astype(BF16)
                run = run + s2[:, ts:]
            for g in range(0, nb, 2):
                grp = a_blocks[g:g + 2]
                a2 = grp[0] if len(grp) == 1 else jnp.concatenate(grp, axis=1)
                acc = acc + _dot(a2, v_ref[0, pl.ds(_aligned(ks + g * ts, ts), len(grp) * ts), :])
            return acc, run

        zero = jnp.zeros((2 * ts, LANES), F32)
        blk = lambda kb: pl.multiple_of(kb * ts, ts)
        gsz = min(4, k_ref.shape[1] // ts)
        last = diag // gsz
        first_start = jnp.minimum(gsz * last, k_ref.shape[1] // ts - gsz)
        carry = step(blk(first_start), (zero, zero), gsz, jnp.maximum(lead, gsz * last * ts))
        carry = lax.fori_loop(1, last, lambda j, c: step(blk(gsz * (last - j)), c, gsz, None), carry)
        carry = _loop_if(last >= 1, lambda c: step(0, c, gsz, lead), carry)
        acc = carry[0]
        o_ref[0, pl.ds(r0, ts), :] = jnp.where(head0, acc[:ts], acc[ts:]).astype(o_ref.dtype)
        return 0

    lax.fori_loop(0, tq // ts, sub_tile, 0)


def _aligned(x, m):
    return x if isinstance(x, int) else pl.multiple_of(x, m)


def _loop_if(pred, f, carry):
    return lax.fori_loop(0, pred.astype(jnp.int32), lambda j, c: f(c), carry)


def _suffix_matrix(tk):
    j = jnp.arange(tk)[:, None]
    s = jnp.arange(tk)[None, :]
    u = jnp.concatenate([(j > s).astype(BF16), jnp.ones((tk, tk), BF16)], axis=1)
    return jnp.concatenate([u, u], axis=0)


def sb_attention(q, k, v, *, tq, lead):
    b, lp, dq = q.shape
    ts = ATTN_BLOCK
    u2 = _suffix_matrix(ts)
    return pl.pallas_call(
        functools.partial(_sb_attn_kernel, tq=tq, ts=ts, lead=lead),
        grid=(b, dq // LANES, lp // tq),
        in_specs=[
            pl.BlockSpec((1, tq, LANES), lambda i, p, j: (i, j, p)),
            pl.BlockSpec((1, lp, LANES), lambda i, p, j: (i, 0, p)),
            pl.BlockSpec((1, lp, LANES), lambda i, p, j: (i, 0, p)),
            pl.BlockSpec(u2.shape, lambda i, p, j: (0, 0)),
        ],
        out_specs=pl.BlockSpec((1, tq, LANES), lambda i, p, j: (i, j, p)),
        out_shape=jax.ShapeDtypeStruct((b, lp, dq), BF16),
        compiler_params=_params(("parallel", "parallel", "arbitrary")),
        name="sb_attention",
    )(q, k, v, u2)


def _fox_attn_kernel(q_ref, k_ref, v_ref, cq_ref, ckt_ref, o_ref, *, tq, ts, lead):
    pair = pl.program_id(1)
    q0 = pl.program_id(2) * tq
    lane = lax.broadcasted_iota(jnp.int32, (1, LANES), 1)
    head0 = lane < HEAD_DIM
    rel = lax.broadcasted_iota(jnp.int32, (ts, ts), 1) - lax.broadcasted_iota(jnp.int32, (ts, ts), 0)
    kcol = lax.broadcasted_iota(jnp.int32, (ts, ts), 1)

    def sub_tile(st, _):
        r0 = pl.multiple_of(st * ts, ts)
        qs0 = q0 + r0
        diag = qs0 // ts
        q = q_ref[0, pl.ds(r0, ts), :]
        cq_all = cq_ref[0, pl.ds(r0, ts), :]
        zq = jnp.zeros_like(q)
        q2 = jnp.concatenate([jnp.where(head0, q, zq), jnp.where(head0, zq, q)], axis=0)
        cq2 = jnp.concatenate([jnp.broadcast_to(
            jnp.sum(jnp.where(lane == pair * 2 + hh, cq_all, 0.0), axis=1, keepdims=True), (ts, ts))
            for hh in range(2)], axis=0)
        def scores(ks, nb, bias, lo):
            masked = lo is not None
            edge = lambda kp: (rel <= (qs0 - kp)) & ((kp + kcol) >= lo)
            zs =[_dot_nt(q2, k_ref[0, pl.ds(_aligned(ks + g * ts, ts), min(2, nb - g) * ts), :]) for g in range(0, nb, 2)]
            tiles = []
            for kk in range(nb):
                ck = jnp.concatenate(
                    [jnp.broadcast_to(ckt_ref[0, hh, :, pl.ds(_aligned(ks + kk * ts, ts), ts)], (ts, ts)) for hh in range(2)],
                    axis=0)
                t = zs[kk // 2][:, (kk % 2) * ts:(kk % 2 + 1) * ts] + bias - ck
                if masked:
                    vis = edge(ks + kk * ts)
                    t = jnp.where(jnp.concatenate([vis, vis], axis=0), t, NEG_INF)
                tiles.append(t)
            return tiles

        blk = lambda kb: pl.multiple_of(kb * ts, ts)
        gsz = min(4, k_ref.shape[1] // ts)
        last = diag // gsz
        last_start = jnp.minimum(gsz * last, k_ref.shape[1] // ts - gsz)

        def sweep(f, carry):
            carry = f(0, carry, gsz, lead)
            carry = lax.fori_loop(1, last, lambda g, c: f(blk(gsz * g), c, gsz, None), carry)
            return _loop_if(last >= 1, lambda c: f(blk(last_start), c, gsz, gsz * last * ts), carry)

        def row_max(ks, mx, nb, lo):
            for t in scores(ks, nb, cq2, lo):
                mx = jnp.maximum(mx, t)
            return mx

        mx = sweep(row_max, jnp.full((2 * ts, ts), NEG_INF, F32))
        m2 = jnp.broadcast_to(jnp.max(mx, axis=1, keepdims=True), (2 * ts, ts))
        cm2 = cq2 - m2

        def accumulate(ks, carry, nb, lo):
            acc, lsum = carry
            if lo is not None:
                ps = [jnp.exp(t - m2) for t in scores(ks, nb, cq2, lo)]
            else:
                ps = [jnp.exp(t) for t in scores(ks, nb, cm2, None)]
            for p in ps:
                lsum = lsum + p
            for g in range(0, nb, 2):
                grp = [p.astype(BF16) for p in ps[g:g + 2]]
                p2 = grp[0] if len(grp) == 1 else jnp.concatenate(grp, axis=1)
                acc = acc + _dot(p2, v_ref[0, pl.ds(_aligned(ks + g * ts, ts), len(grp) * ts), :])
            return acc, lsum

        zero = jnp.zeros((2 * ts, LANES), F32)
        acc, lsum = sweep(accumulate, (zero, zero))
        out = acc / jnp.sum(lsum, axis=1, keepdims=True)
        o_ref[0, pl.ds(r0, ts), :] = jnp.where(head0, out[:ts], out[ts:]).astype(o_ref.dtype)
        return 0

    lax.fori_loop(0, tq // ts, sub_tile, 0)


def fox_attention(q, k, v, cum, cum_t, *, tq, lead):
    b, lp, dq = q.shape
    return pl.pallas_call(
        functools.partial(_fox_attn_kernel, tq=tq, ts=ATTN_BLOCK, lead=lead),
        grid=(b, dq // LANES, lp // tq),
        in_specs=[
            pl.BlockSpec((1, tq, LANES), lambda i, p, j: (i, j, p)),
            pl.BlockSpec((1, lp, LANES), lambda i, p, j: (i, 0, p)),
            pl.BlockSpec((1, lp, LANES), lambda i, p, j: (i, 0, p)),
            pl.BlockSpec((1, tq, LANES), lambda i, p, j: (i, j, 0)),
            pl.BlockSpec((1, 2, 1, lp), lambda i, p, j: (i, p, 0, 0)),
        ],
        out_specs=pl.BlockSpec((1, tq, LANES), lambda i, p, j: (i, j, p)),
        out_shape=jax.ShapeDtypeStruct((b, lp, dq), BF16),
        compiler_params=_params(("parallel", "parallel", "arbitrary")),
        name="fox_attention",
    )(q, k, v, cum, cum_t)


def _logf_cum_kernel(fl_ref, bf_ref, logf_ref, cum_ref, cumt_ref, carry_ref, carryt_ref, *, lead):
    j = pl.program_id(1)

    @pl.when(j == 0)
    def _():
        carry_ref[...] = jnp.zeros_like(carry_ref)
        carryt_ref[...] = jnp.zeros_like(carryt_ref)

    x = fl_ref[0] + bf_ref[...]
    logf = jnp.minimum(x, 0.0) - jnp.log1p(jnp.exp(-jnp.abs(x)))
    logf_ref[0] = logf
    tb = logf.shape[0]
    pos = j * tb + lax.broadcasted_iota(jnp.int32, (tb, 1), 0)
    lz = jnp.where(pos >= lead, logf, 0.0)
    ri = lax.broadcasted_iota(jnp.int32, (tb, tb), 0)
    ci = lax.broadcasted_iota(jnp.int32, (tb, tb), 1)
    tril = (ci <= ri).astype(BF16)
    triu = (ri <= ci).astype(BF16)
    c = carry_ref[...]
    ct = carryt_ref[...]
    for p in _split3(lz):
        c = c + _dot(tril, p)
        ct = ct + _dot_tn(p, triu)
    cum_ref[0] = c
    cumt_ref[0] = ct
    carry_ref[...] = jnp.broadcast_to(c[tb - 1:tb, :], carry_ref.shape)
    carryt_ref[...] = jnp.broadcast_to(ct[:, tb - 1:tb], carryt_ref.shape)


def logf_cumsum(fl, b_f, *, lead):
    b, lp, _ = fl.shape
    tb = LANES
    return pl.pallas_call(
        functools.partial(_logf_cum_kernel, lead=lead),
        grid=(b, lp // tb),
        in_specs=[
            pl.BlockSpec((1, tb, LANES), lambda i, j: (i, j, 0)),
            pl.BlockSpec((1, LANES), lambda i, j: (0, 0)),
        ],
        out_specs=[
            pl.BlockSpec((1, tb, LANES), lambda i, j: (i, j, 0)),
            pl.BlockSpec((1, tb, LANES), lambda i, j: (i, j, 0)),
            pl.BlockSpec((1, LANES, tb), lambda i, j: (i, 0, j)),
        ],
        out_shape=[
            jax.ShapeDtypeStruct((b, lp, LANES), F32),
            jax.ShapeDtypeStruct((b, lp, LANES), F32),
            jax.ShapeDtypeStruct((b, LANES, lp), F32),
        ],
        scratch_shapes=[pltpu.VMEM((tb, LANES), F32), pltpu.VMEM((LANES, tb), F32)],
        compiler_params=_params(("parallel", "arbitrary")),
        name="logf_cumsum",
    )(fl, b_f)


def _pair_masks(rows):
    lane = lax.broadcasted_iota(jnp.int32, (rows, LANES), 1)
    return lane < HEAD_DIM, lane >= HEAD_DIM


def _bd(x):
    m0, m1 = _pair_masks(x.shape[0])
    zero = jnp.zeros_like(x)
    return jnp.concatenate([jnp.where(m0, x, zero), jnp.where(m1, x, zero)], axis=0)


def _rwkv_pair_chunk(r, k, v, av, b, ld, bds, sums=None):
    c = r.shape[0]
    cs, tot = _decay_sums(ld) if sums is None else sums
    e_pos = jnp.exp(cs)
    e_neg = jnp.exp(-cs)
    at = (av * jnp.exp(cs - ld)).astype(BF16)
    rt = (r * e_pos).astype(BF16)
    bt = b * e_neg
    kt = k * e_neg
    e_end = jnp.exp(cs[c - 1:c, :] - cs)
    bh = (b * e_end).astype(BF16)
    kh = (k * e_end).astype(BF16)

    lhs = jnp.concatenate([at, rt], axis=0)
    rhs = jnp.concatenate([_bd(bt), _bd(kt)], axis=0).astype(BF16)
    sc = _dot_nt(lhs, rhs)
    t2 = lax.broadcasted_iota(jnp.int32, (c, 2 * c), 0)
    s2 = lax.broadcasted_iota(jnp.int32, (c, 2 * c), 1) & (c - 1)
    strict = s2 < t2
    incl = s2 <= t2
    a_ab = jnp.where(strict, sc[:c, :2 * c], 0.0)
    a_ak = jnp.where(strict, sc[:c, 2 * c:], 0.0)
    a_rb = jnp.where(incl, sc[c:, :2 * c], 0.0)
    a_rk = jnp.where(incl, sc[c:, 2 * c:], 0.0)

    inv = jnp.where(s2 == t2, 1.0, 0.0) + a_ab
    lp = a_ab
    n_sq = max(c.bit_length() - 2, 0)
    for _ in range(n_sq):
        lp = _dot(lp.astype(BF16), _bd(lp).astype(BF16))
        inv = inv + _dot(inv.astype(BF16), _bd(lp).astype(BF16))

    bds16 = bds.astype(BF16)
    vbd = _bd(v).astype(BF16)
    rhs_u = _dot(at, bds16) + _dot(a_ak.astype(BF16), vbd)
    u = _dot(inv.astype(BF16), _bd(rhs_u).astype(BF16))
    ubd = _bd(u).astype(BF16)
    y = (_dot(rt, bds16)
         + _dot(jnp.concatenate([a_rb, a_rk], axis=1).astype(BF16), jnp.concatenate([ubd, vbd], axis=0)))
    cross = _dot_tn(bh, u.astype(BF16)) + _dot_tn(kh, v.astype(BF16))
    bds_new = jnp.exp(tot) * bds + jnp.where(_same_head(LANES, LANES), cross, 0.0)
    return y, bds_new


def _same_head(rows, cols):
    shift = HEAD_DIM.bit_length() - 1
    rr = lax.broadcasted_iota(jnp.int32, (rows, cols), 0) >> shift
    cc = lax.broadcasted_iota(jnp.int32, (rows, cols), 1) >> shift
    return rr == cc


def _head_sum(x, scale=1.0):
    w = x.shape[1]
    g = jnp.where(_same_head(w, w), scale, 0.0).astype(BF16)
    hi, lo = _split2(x)
    return _dot(hi, g) + _dot(lo, g)


def _decay_sums(ld):
    c, w = ld.shape
    tril = (lax.broadcasted_iota(jnp.int32, (c, c), 1) <= lax.broadcasted_iota(jnp.int32, (c, c), 0)).astype(BF16)
    ones = jnp.ones((c, LANES), BF16)
    cs = jnp.zeros((c, w), F32)
    tot = jnp.zeros((w, LANES), F32)
    for part in _split3(ld):
        cs = cs + _dot(tril, part)
        tot = tot + _dot_tn(part, ones)
    return cs, tot


def _softplus(x):
    return jnp.maximum(x, 0.0) + jnp.log1p(jnp.exp(-jnp.abs(x)))


def _rwkv_prep(ps, w0, w2p, a0, a2p, g2):
    da = w0.shape[1]
    x12 = ps[:, 3 * da:3 * da + LANES]
    lg = ps[:, 3 * da + LANES:3 * da + 2 * LANES]
    w_log = -_softplus(-(w0 + _dot(jnp.tanh(x12).astype(BF16), w2p))) - 0.5
    ld = -jnp.exp(w_log)
    a = jax.nn.sigmoid(a0 + _dot(x12.astype(BF16), a2p))
    g = _dot(jax.nn.sigmoid(lg).astype(BF16), g2)
    return ld, a, g


def _rwkv_pair_inputs(ps, a, kk_w, ka_w, pr):
    da = a.shape[1]
    sl = slice(LANES * pr, LANES * (pr + 1))
    r = ps[:, sl]
    k = ps[:, da + LANES * pr:da + LANES * (pr + 1)]
    v = ps[:, 2 * da + LANES * pr:2 * da + LANES * (pr + 1)]
    kk = k * kk_w[:, sl]
    kk = kk / jnp.maximum(jnp.sqrt(_head_sum(kk * kk)), 1e-12)
    ap = a[:, sl]
    k2 = k * (1.0 + (ap - 1.0) * ka_w[:, sl])
    return r, k2, v, kk, ap


def _rwkv_pair_output(y, r, k2, v, g, rk_w, lnw, lnb, pr):
    sl = slice(LANES * pr, LANES * (pr + 1))
    mean = _head_sum(y, 1.0 / HEAD_DIM)
    d = y - mean
    var = _head_sum(d * d, 1.0 / HEAD_DIM)
    yn = d * lax.rsqrt(var + GN_EPS) * lnw[:, sl] + lnb[:, sl]
    bonus = _head_sum(r * k2 * rk_w[:, sl]) * v
    return (yn + bonus) * g[:, sl]


def _rwkv_mix_kernel(p_ref, mu_ref, w0_ref, w2_ref, a0_ref, a2_ref, g2_ref, kk_ref, ka_ref, rk_ref,
                     lnw_ref, lnb_ref, ya_ref, sfin_ref, prev_ref, st_ref, *, c, n_pairs):
    j = pl.program_id(1)

    @pl.when(j == 0)
    def _():
        prev_ref[...] = jnp.zeros_like(prev_ref)
        st_ref[...] = jnp.zeros_like(st_ref)

    p = p_ref[0]
    row = lax.broadcasted_iota(jnp.int32, (c, 1), 0)
    p_prev = jnp.where(row == 0, prev_ref[0:1, :], pltpu.roll(p, 1, 0))
    prev_ref[...] = jnp.broadcast_to(p[c - 1:c, :], prev_ref.shape)
    ps = p + (p_prev - p) * mu_ref[...]
    ld, a, g = _rwkv_prep(ps, w0_ref[...], w2_ref[...], a0_ref[...], a2_ref[...], g2_ref[...])
    da = a.shape[1]
    r, k, v = ps[:, :da], ps[:, da:2 * da], ps[:, 2 * da:3 * da]
    kk = k * kk_ref[...]
    kk = kk / jnp.maximum(jnp.sqrt(_head_sum(kk * kk)), 1e-12)
    k2 = k * (1.0 + (a - 1.0) * ka_ref[...])
    bonus = _head_sum(r * k2 * rk_ref[...]) * v
    kka = kk * a
    cs, tot = _decay_sums(ld)
    ys = []
    for pr in range(n_pairs):
        sl = slice(LANES * pr, LANES * (pr + 1))
        y, bds = _rwkv_pair_chunk(r[:, sl], k2[:, sl], v[:, sl], -kk[:, sl], kka[:, sl], ld[:, sl], st_ref[pr],
                                  sums=(cs[:, sl], tot[LANES * pr:LANES * (pr + 1), :]))
        st_ref[pr] = bds
        ys.append(y)
    y = jnp.concatenate(ys, axis=1)
    d = y - _head_sum(y, 1.0 / HEAD_DIM)
    var = _head_sum(d * d, 1.0 / HEAD_DIM)
    yn = d * lax.rsqrt(var + GN_EPS) * lnw_ref[...] + lnb_ref[...]
    ya_ref[0] = ((yn + bonus) * g).astype(ya_ref.dtype)

    @pl.when(j == pl.num_programs(1) - 1)
    def _():
        sfin_ref[0] = st_ref[...]


def rwkv_mix_prompt(p, mu, w0, w2p, a0, a2p, g2, kk_w, ka_w, rk_w, lnw, lnb):
    b, lp, ds = p.shape
    da = w0.shape[1]
    n_pairs = da // LANES
    c = RWKV_CHUNK
    vec = lambda n: pl.BlockSpec((1, n), lambda i, j: (0, 0))
    mat = lambda w: pl.BlockSpec(w.shape, lambda i, j: (0, 0))
    return pl.pallas_call(
        functools.partial(_rwkv_mix_kernel, c=c, n_pairs=n_pairs),
        grid=(b, lp // c),
        in_specs=[pl.BlockSpec((1, c, ds), lambda i, j: (i, j, 0)), vec(ds), vec(da), mat(w2p), vec(da), mat(a2p),
                  mat(g2), vec(da), vec(da), vec(da), vec(da), vec(da)],
        out_specs=[pl.BlockSpec((1, c, da), lambda i, j: (i, j, 0)),
                   pl.BlockSpec((1, n_pairs, LANES, LANES), lambda i, j: (i, 0, 0, 0))],
        out_shape=[jax.ShapeDtypeStruct((b, lp, da), BF16),
                   jax.ShapeDtypeStruct((b, n_pairs, LANES, LANES), F32)],
        scratch_shapes=[pltpu.VMEM((8, ds), F32), pltpu.VMEM((n_pairs, LANES, LANES), F32)],
        compiler_params=_params(("parallel", "arbitrary")),
        name="rwkv_mix_prompt",
    )(p, mu, w0, w2p, a0, a2p, g2, kk_w, ka_w, rk_w, lnw, lnb)


def _rwkv_step_kernel(p_ref, prev_ref, s_ref, mu_ref, w0_ref, w2_ref, a0_ref, a2_ref, g2_ref, kk_ref, ka_ref,
                      rk_ref, lnw_ref, lnb_ref, ya_ref, snew_ref, r_s, w_s, k_s, v_s, a_s, b_s, y_s,
                      *, n_pairs):
    p = p_ref[...]
    ps = p + (prev_ref[...] - p) * mu_ref[...]
    ld, a, g = _rwkv_prep(ps, w0_ref[...], w2_ref[...], a0_ref[...], a2_ref[...], g2_ref[...])
    pair_vals = []
    for pr in range(n_pairs):
        sl = slice(LANES * pr, LANES * (pr + 1))
        r, k2, v, kk, ap = _rwkv_pair_inputs(ps, a, kk_ref[...], ka_ref[...], pr)
        pair_vals.append((r, k2, v))
        r_s[:, sl] = r
        w_s[:, sl] = jnp.exp(ld[:, sl])
        k_s[:, sl] = k2
        v_s[:, sl] = v
        a_s[:, sl] = -kk
        b_s[:, sl] = kk * ap

    hd = HEAD_DIM
    eye = lax.broadcasted_iota(jnp.int32, (hd, hd), 0) == lax.broadcasted_iota(jnp.int32, (hd, hd), 1)
    sel = (lax.broadcasted_iota(jnp.int32, (hd, LANES), 1) & (hd - 1)) == lax.broadcasted_iota(
        jnp.int32, (hd, LANES), 0)
    lane = lax.broadcasted_iota(jnp.int32, (1, LANES), 1)

    def body(bg, carry):
        b0 = pl.multiple_of(bg * 8, 8)
        for pr in range(n_pairs):
            sl = slice(LANES * pr, LANES * (pr + 1))
            tiles = {}
            for name, ref in (("a", a_s), ("b", b_s), ("w", w_s), ("k", k_s), ("r", r_s), ("v", v_s)):
                x = ref[pl.ds(b0, 8), sl]
                tiles[name] = (x, pltpu.roll(x, hd, 1))
            y_rows = []
            for i in range(8):
                heads = []
                for hh in range(2):
                    vec = lambda name: tiles[name][hh][i:i + 1, :hd]
                    s = s_ref[b0 + i, 2 * pr + hh]
                    sa = jnp.sum(s * vec("a"), axis=1, keepdims=True)
                    v_col = jnp.sum(jnp.where(eye, vec("v"), 0.0), axis=1, keepdims=True)
                    s2 = s * vec("w") + sa * vec("b") + v_col * vec("k")
                    snew_ref[b0 + i, 2 * pr + hh] = s2
                    y_col = jnp.sum(s2 * vec("r"), axis=1, keepdims=True)
                    heads.append(jnp.sum(jnp.where(sel, y_col, 0.0), axis=0, keepdims=True))
                y_rows.append(jnp.where(lane < hd, heads[0], heads[1]))
            y_s[pl.ds(b0, 8), sl] = jnp.concatenate(y_rows, axis=0)
        return carry

    lax.fori_loop(0, p.shape[0] // 8, body, 0)
    y = y_s[...]
    for pr in range(n_pairs):
        sl = slice(LANES * pr, LANES * (pr + 1))
        r, k2, v = pair_vals[pr]
        out = _rwkv_pair_output(y[:, sl], r, k2, v, g, rk_ref[...], lnw_ref[...], lnb_ref[...], pr)
        ya_ref[:, sl] = out.astype(ya_ref.dtype)


def rwkv_step(p, prev, state, mu, w0, w2p, a0, a2p, g2, kk_w, ka_w, rk_w, lnw, lnb):
    db, ds = p.shape
    da = w0.shape[1]
    return pl.pallas_call(
        functools.partial(_rwkv_step_kernel, n_pairs=da // LANES),
        out_shape=[jax.ShapeDtypeStruct((db, da), BF16), jax.ShapeDtypeStruct(state.shape, F32)],
        scratch_shapes=[pltpu.VMEM((db, da), F32) for _ in range(7)],
        compiler_params=pltpu.CompilerParams(vmem_limit_bytes=VMEM_LIMIT),
        name="rwkv_step",
    )(p, prev, state, mu, w0, w2p, a0, a2p, g2, kk_w, ka_w, rk_w, lnw, lnb)


def _head_rows(q_row, rows):
    width = q_row.shape[1]
    row = lax.broadcasted_iota(jnp.int32, (rows, width), 0)
    col = lax.broadcasted_iota(jnp.int32, (rows, width), 1) >> 6
    return jnp.where(row == col, q_row, 0.0)


def _diag_blocks(acc):
    row = lax.broadcasted_iota(jnp.int32, acc.shape, 0)
    col = lax.broadcasted_iota(jnp.int32, acc.shape, 1) >> 6
    return jnp.sum(jnp.where(row == col, acc, 0.0), axis=0, keepdims=True)


def _log_sigmoid(x):
    return jnp.minimum(x, 0.0) - jnp.log1p(jnp.exp(-jnp.abs(x)))


def _sb_decode_kernel(pt_ref, q_ref, kn_ref, vn_ref, u2_ref, *rest, pp, n_pages, n_heads):
    k_refs, v_refs = rest[:pp], rest[pp:2 * pp]
    o_ref, qt_ref, acc_ref, run_ref = rest[2 * pp:]
    j = pl.program_id(1)
    qpos = n_pages * PAGE_SIZE
    lane = lax.broadcasted_iota(jnp.int32, (1, PAGE_SIZE), 1)

    def page(kpg, vpg, kpos0, n_valid):
        z = _dot(qt_ref[...], kpg.astype(BF16))[:n_heads]
        vis = ((kpos0 + lane) < qpos) & (lane < n_valid)
        ls = _log_sigmoid(z)
        lr = jnp.where(vis, ls - z, 0.0)
        hi, lo = _split2(lr)
        s2 = _dot(jnp.concatenate([hi, lo], axis=1), u2_ref[...])
        a = jnp.where(vis, jnp.exp(ls + s2[:, :PAGE_SIZE] + run_ref[...]), 0.0)
        acc_ref[...] += _dot_nt(a.astype(BF16), vpg.astype(BF16))
        run_ref[...] += s2[:, PAGE_SIZE:]

    @pl.when(j == 0)
    def _():
        qt_ref[...] = _head_rows(q_ref[0].astype(F32), qt_ref.shape[0]).astype(BF16)
        acc_ref[...] = jnp.zeros_like(acc_ref)
        run_ref[...] = jnp.zeros_like(run_ref)
        page(kn_ref[0], vn_ref[0], qpos, 1)

    def page_pair(k_a, v_a, k_b, v_b, kpos_a):
        h = n_heads
        z = _dot(qt_ref[...], jnp.concatenate([k_a, k_b], axis=1).astype(BF16))[:h]
        ls, lr, vis = [], [], []
        for i, kp in enumerate((kpos_a, kpos_a - PAGE_SIZE)):
            zi = z[:, i * PAGE_SIZE:(i + 1) * PAGE_SIZE]
            vis.append((kp + lane) < qpos)
            ls.append(_log_sigmoid(zi))
            lr.append(jnp.where(vis[i], ls[i] - zi, 0.0))
        hi, lo = _split2(jnp.concatenate(lr, axis=0))
        s2 = _dot(jnp.concatenate([hi, lo], axis=1), u2_ref[...])
        run = run_ref[...]
        tot_a = s2[:h, PAGE_SIZE:]
        a_a = jnp.where(vis[0], jnp.exp(ls[0] + s2[:h, :PAGE_SIZE] + run), 0.0)
        a_b = jnp.where(vis[1], jnp.exp(ls[1] + s2[h:, :PAGE_SIZE] + (run + tot_a)), 0.0)
        acc_ref[...] += _dot_nt(jnp.concatenate([a_a, a_b], axis=1).astype(BF16),
                                jnp.concatenate([v_a, v_b], axis=1).astype(BF16))
        run_ref[...] = run + tot_a + s2[h:, PAGE_SIZE:]

    for s in range(0, pp - 1, 2):
        page_pair(k_refs[s][0, 0], v_refs[s][0, 0], k_refs[s + 1][0, 0], v_refs[s + 1][0, 0],
                  (n_pages - 1 - (j * pp + s)) * PAGE_SIZE)
    if pp % 2:
        page(k_refs[pp - 1][0, 0], v_refs[pp - 1][0, 0], (n_pages - 1 - (j * pp + pp - 1)) * PAGE_SIZE, PAGE_SIZE)

    @pl.when(j == pl.num_programs(1) - 1)
    def _():
        o_ref[0] = _diag_blocks(acc_ref[...]).astype(o_ref.dtype)


def _page_specs(layer, pp, n_pages, width):
    def spec(s):
        return pl.BlockSpec((1, 1, width, PAGE_SIZE),
                            lambda b, j, pt: (layer, pt[b, n_pages - 1 - (j * pp + s)], 0, 0))
    return [spec(s) for s in range(pp)]


def _cache_view(cache):
    nl, n_pool, rows, n_heads, hd = cache.shape
    return jnp.transpose(cache, (0, 1, 3, 4, 2)).reshape(nl, n_pool, n_heads * hd, rows)


def _new_token_page(x):
    return jnp.pad(x[:, :, None], ((0, 0), (0, 0), (0, PAGE_SIZE - 1)))


def sb_decode(q, k_new, v_new, cache_k, cache_v, layer, page_table, *, pp):
    db, _, d = q.shape
    n_pages = page_table.shape[1]
    n_heads = d // HEAD_DIM
    u2 = _suffix_matrix(PAGE_SIZE)
    row = lambda r, w: pl.BlockSpec((1, r, w), lambda b, j, pt: (b, 0, 0))
    grid_spec = pltpu.PrefetchScalarGridSpec(
        num_scalar_prefetch=1,
        grid=(db, n_pages // pp),
        in_specs=[row(1, d), row(d, PAGE_SIZE), row(d, PAGE_SIZE),
                  pl.BlockSpec(u2.shape, lambda b, j, pt: (0, 0))]
                 + _page_specs(layer, pp, n_pages, d) + _page_specs(layer, pp, n_pages, d),
        out_specs=row(1, d),
        scratch_shapes=[pltpu.VMEM((16, d), BF16), pltpu.VMEM((n_heads, d), F32),
                        pltpu.VMEM((n_heads, PAGE_SIZE), F32)],
    )
    return pl.pallas_call(
        functools.partial(_sb_decode_kernel, pp=pp, n_pages=n_pages, n_heads=n_heads),
        grid_spec=grid_spec,
        out_shape=jax.ShapeDtypeStruct((db, 1, d), BF16),
        compiler_params=_params(("parallel", "arbitrary")),
        name="sb_decode",
    )(page_table, q, k_new, v_new, u2, *([cache_k] * pp), *([cache_v] * pp))


def _fox_decode_kernel(pt_ref, q_ref, kn_ref, vn_ref, fl_ref, bf_ref, m2_ref, *rest, pp, n_pages, n_heads):
    k_refs, v_refs, lf_refs = rest[:pp], rest[pp:2 * pp], rest[2 * pp:3 * pp]
    o_ref, lfo_ref, qt_ref, acc_ref, m_ref, l_ref, run_ref = rest[3 * pp:]
    j = pl.program_id(1)
    qpos = n_pages * PAGE_SIZE
    lane = lax.broadcasted_iota(jnp.int32, (1, PAGE_SIZE), 1)

    def page(kpg, vpg, bias, kpos0, n_valid):
        z = _dot(qt_ref[...], kpg.astype(BF16))[:n_heads] + bias
        vis = ((kpos0 + lane) <= qpos) & (lane < n_valid)
        z = jnp.where(vis, z, NEG_INF)
        m_new = jnp.maximum(m_ref[...], jnp.max(z, axis=1, keepdims=True))
        alpha = jnp.exp(m_ref[...] - m_new)
        p = jnp.exp(z - m_new)
        l_ref[...] = alpha * l_ref[...] + jnp.sum(p, axis=1, keepdims=True)
        acc_ref[...] = alpha * acc_ref[...] + _dot_nt(p.astype(BF16), vpg.astype(BF16))
        m_ref[...] = m_new

    @pl.when(j == 0)
    def _():
        qt_ref[...] = _head_rows(q_ref[0].astype(F32), qt_ref.shape[0]).astype(BF16)
        acc_ref[...] = jnp.zeros_like(acc_ref)
        m_ref[...] = jnp.full_like(m_ref, NEG_INF)
        l_ref[...] = jnp.zeros_like(l_ref)
        logf = _log_sigmoid(fl_ref[0] + bf_ref[...])
        lfo_ref[0] = logf
        eye = lax.broadcasted_iota(jnp.int32, (LANES, LANES), 0) == lax.broadcasted_iota(
            jnp.int32, (LANES, LANES), 1)
        col = jnp.sum(jnp.where(eye, logf, 0.0), axis=1, keepdims=True)
        run_ref[...] = jnp.broadcast_to(col[:n_heads], run_ref.shape)
        page(kn_ref[0], vn_ref[0], 0.0, qpos, 1)

    def page_pair(s):
        h = n_heads
        kpos_a = (n_pages - 1 - (j * pp + s)) * PAGE_SIZE
        parts = _split3(jnp.concatenate([lf_refs[s][0], lf_refs[s + 1][0]], axis=0))
        s6 = _dot(jnp.concatenate(parts, axis=0), m2_ref[...])
        s2 = s6[:2 * h] + s6[2 * h:4 * h] + s6[4 * h:]
        run = run_ref[...]
        tot_a = s2[:h, PAGE_SIZE:]
        bias = jnp.concatenate([run + s2[:h, :PAGE_SIZE], (run + tot_a) + s2[h:, :PAGE_SIZE]], axis=1)
        k2 = jnp.concatenate([k_refs[s][0, 0], k_refs[s + 1][0, 0]], axis=1).astype(BF16)
        v2 = jnp.concatenate([v_refs[s][0, 0], v_refs[s + 1][0, 0]], axis=1).astype(BF16)
        kpos = jnp.concatenate([kpos_a + lane, kpos_a - PAGE_SIZE + lane], axis=1)
        z = jnp.where(kpos <= qpos, _dot(qt_ref[...], k2)[:h] + bias, NEG_INF)
        m_new = jnp.maximum(m_ref[...], jnp.max(z, axis=1, keepdims=True))
        alpha = jnp.exp(m_ref[...] - m_new)
        p = jnp.exp(z - m_new)
        l_ref[...] = alpha * l_ref[...] + jnp.sum(p, axis=1, keepdims=True)
        acc_ref[...] = alpha * acc_ref[...] + _dot_nt(p.astype(BF16), v2)
        m_ref[...] = m_new
        run_ref[...] = run + tot_a + s2[h:, PAGE_SIZE:]

    for s in range(0, pp - 1, 2):
        page_pair(s)
    if pp % 2:
        s = pp - 1
        lf = lf_refs[s][0]
        s2 = jnp.zeros((n_heads, 2 * PAGE_SIZE), F32)
        for part in _split3(lf):
            s2 = s2 + _dot(part, m2_ref[...])
        page(k_refs[s][0, 0], v_refs[s][0, 0], run_ref[...] + s2[:, :PAGE_SIZE],
             (n_pages - 1 - (j * pp + s)) * PAGE_SIZE, PAGE_SIZE)
        run_ref[...] += s2[:, PAGE_SIZE:]

    @pl.when(j == pl.num_programs(1) - 1)
    def _():
        o_ref[0] = _diag_blocks(acc_ref[...] / l_ref[...]).astype(o_ref.dtype)


def fox_decode(q, k_new, v_new, fl_new, b_f, cache_k, cache_v, layer, cache_lf_t, page_table, *, pp):
    db, _, d = q.shape
    n_pages = page_table.shape[1]
    n_heads = d // HEAD_DIM
    jj = jnp.arange(PAGE_SIZE)[:, None]
    ss = jnp.arange(PAGE_SIZE)[None, :]
    m2 = jnp.concatenate([(jj > ss).astype(BF16), jnp.ones((PAGE_SIZE, PAGE_SIZE), BF16)], axis=1)
    row = lambda r, w: pl.BlockSpec((1, r, w), lambda b, j, pt: (b, 0, 0))
    lf_specs = [pl.BlockSpec((1, n_heads, PAGE_SIZE),
                             lambda b, j, pt, s=s: (pt[b, n_pages - 1 - (j * pp + s)], 0, 0)) for s in range(pp)]
    grid_spec = pltpu.PrefetchScalarGridSpec(
        num_scalar_prefetch=1,
        grid=(db, n_pages // pp),
        in_specs=[row(1, d), row(d, PAGE_SIZE), row(d, PAGE_SIZE), row(1, LANES),
                  pl.BlockSpec((1, LANES), lambda b, j, pt: (0, 0)),
                  pl.BlockSpec(m2.shape, lambda b, j, pt: (0, 0))]
                 + _page_specs(layer, pp, n_pages, d) + _page_specs(layer, pp, n_pages, d) + lf_specs,
        out_specs=[row(1, d), row(1, LANES)],
        scratch_shapes=[pltpu.VMEM((16, d), BF16), pltpu.VMEM((n_heads, d), F32),
                        pltpu.VMEM((n_heads, 1), F32), pltpu.VMEM((n_heads, 1), F32),
                        pltpu.VMEM((n_heads, PAGE_SIZE), F32)],
    )
    return pl.pallas_call(
        functools.partial(_fox_decode_kernel, pp=pp, n_pages=n_pages, n_heads=n_heads),
        grid_spec=grid_spec,
        out_shape=[jax.ShapeDtypeStruct((db, 1, d), BF16), jax.ShapeDtypeStruct((db, 1, LANES), F32)],
        compiler_params=_params(("parallel", "arbitrary")),
        name="fox_decode",
    )(page_table, q, k_new, v_new, fl_new, b_f, m2, *([cache_k] * pp), *([cache_v] * pp), *([cache_lf_t] * pp))


def _pad_lanes(x):
    return jnp.pad(x, [(0, 0)] * (x.ndim - 1) + [(0, LANES - x.shape[-1])])


def kernel(x_prompt, x_sample, cache_sb_k, cache_sb_v, cache_fox_k, cache_fox_v, cache_fox_logf, state_rwkv, state_rwkv_shift, page_table, meta_tokens, norm_g, ffn_w1, ffn_w3, ffn_w2, w_in_even, w_out_even, rwkv_mu, rwkv_w0, rwkv_w2, rwkv_a0, rwkv_a2, rwkv_g2, rwkv_kk, rwkv_ka, rwkv_rk, rwkv_lnw, rwkv_lnb, w_in_odd, b_f, w_out_odd):
    bsz, seq, d = x_prompt.shape
    db, ts, _ = x_sample.shape
    assert ts == 1, "the decode kernels handle one new token per sequence"
    depth = norm_g.shape[0]
    t_p = N_META + seq
    lp = LEAD + t_p
    n_pool, n_pages = cache_sb_k.shape[1], page_table.shape[1]
    h_a = state_rwkv.shape[2]
    d_a = h_a * HEAD_DIM
    d_b = cache_sb_k.shape[3] * HEAD_DIM
    d_c = cache_fox_k.shape[3] * HEAD_DIM
    h_c = d_c // HEAD_DIM
    d_shift = state_rwkv_shift.shape[2]
    dff = ffn_w1.shape[3]

    hp = jnp.concatenate([jnp.zeros((bsz, LEAD, d), F32),
                          jnp.broadcast_to(meta_tokens[None].astype(F32), (bsz, N_META, d)), x_prompt], axis=1)
    hs = x_sample.reshape(db, d)

    tm = _row_tile(lp)
    tq = tm
    tf = dff // 2 if (dff // 2) % LANES == 0 else dff
    pp = min(8, n_pages)
    gvec = lambda l, i: norm_g[l, i][None, :]

    def ffn(h2, l, i):
        w1, w3, w2 = (w[l, i].astype(BF16) for w in (ffn_w1, ffn_w3, ffn_w2))
        return ffn_half(h2, gvec(l, 2 * i * 2), gvec(l, 2 * i * 2 + 1), w1, w3, w2,
                        tm=_row_tile(h2.shape[0]), tf=tf)

    outs_p = {k: [] for k in ("sb_k", "sb_v", "fk", "fv", "fl", "rw", "sh")}
    outs_s = {k: [] for k in ("sb_k", "sb_v", "fk", "fv", "fl", "rw", "sh")}

    for l in range(depth):
        hp = ffn(hp.reshape(bsz * lp, d), l, 0).reshape(bsz, lp, d)
        hs = ffn(hs, l, 0)
        g_mix, g_post = gvec(l, 2), gvec(l, 3)
        if l % 2 == 0:
            e = l // 2
            w_in = w_in_even[e].astype(BF16)
            slabs = [w_in[:, :d_shift], w_in[:, d_shift:d_shift + d_b],
                     w_in[:, d_shift + d_b:d_shift + 2 * d_b], w_in[:, d_shift + 2 * d_b:]]
            dts = [(F32,), (BF16,), (F32, BF16), (F32, BF16)]
            scl = [1.0, HEAD_DIM ** -0.5, 1.0, 1.0]
            zeros_l = jnp.zeros((LORA_W, d_a), F32)
            rw = (rwkv_mu[e][None], rwkv_w0[e][None],
                  jnp.concatenate([rwkv_w2[e], zeros_l], axis=0).astype(BF16), rwkv_a0[e][None],
                  jnp.concatenate([zeros_l, rwkv_a2[e]], axis=0).astype(BF16), rwkv_g2[e].astype(BF16),
                  rwkv_kk[e][None], rwkv_ka[e][None], rwkv_rk[e].reshape(1, d_a), rwkv_lnw[e][None],
                  rwkv_lnb[e][None])
            w_out = w_out_even[e].astype(BF16)
            p, q, k, kb, v, vb = norm_proj(hp.reshape(bsz * lp, d), g_mix, slabs, dts, scl, tm=tm)
            r3 = lambda x: x.reshape(bsz, lp, x.shape[-1])
            ya, sfin = rwkv_mix_prompt(r3(p), *rw)
            yb = sb_attention(r3(q), r3(kb), r3(vb), tq=tq, lead=LEAD)
            hp = out_proj(hp, g_post, [ya, yb], [w_out[:d_a], w_out[d_a:]], tm=tm, lead=LEAD)
            outs_p["sb_k"].append(r3(k)[:, LEAD:].reshape(bsz, t_p, -1, HEAD_DIM))
            outs_p["sb_v"].append(r3(v)[:, LEAD:].reshape(bsz, t_p, -1, HEAD_DIM))
            s_pairs = sfin.reshape(bsz, d_a // LANES, 2, HEAD_DIM, 2, HEAD_DIM)
            s_heads = jnp.stack([s_pairs[:, :, 0, :, 0, :], s_pairs[:, :, 1, :, 1, :]], axis=2)
            outs_p["rw"].append(jnp.swapaxes(s_heads.reshape(bsz, h_a, HEAD_DIM, HEAD_DIM), -1, -2))
            outs_p["sh"].append(r3(p)[:, -1])
            p, q, k, v = norm_proj(hs, g_mix, slabs, [(F32,), (BF16,), (F32,), (F32,)], scl, tm=_row_tile(db))
            ya, s_new = rwkv_step(p, state_rwkv_shift[e], state_rwkv[e], *rw)
            yb = sb_decode(q[:, None], _new_token_page(k), _new_token_page(v), _cache_view(cache_sb_k),
                           _cache_view(cache_sb_v), e, page_table, pp=pp)
            hs = out_proj(hs[None], g_post, [ya[None], yb.reshape(1, db, d_b)], [w_out[:d_a], w_out[d_a:]],
                          tm=_row_tile(db), lead=0)[0]
            outs_s["sb_k"].append(k.reshape(db, 1, -1, HEAD_DIM))
            outs_s["sb_v"].append(v.reshape(db, 1, -1, HEAD_DIM))
            outs_s["rw"].append(s_new)
            outs_s["sh"].append(p)
        else:
            o = l // 2
            w_in = w_in_odd[o]
            slabs = [w_in[:, :d_c].astype(BF16), w_in[:, d_c:2 * d_c].astype(BF16),
                     w_in[:, 2 * d_c:3 * d_c].astype(BF16), _pad_lanes(w_in[:, 3 * d_c:]).astype(BF16)]
            dts = [(BF16,), (F32, BF16), (F32, BF16), (F32,)]
            scl = [HEAD_DIM ** -0.5, 1.0, 1.0, 1.0]
            bf_row = _pad_lanes(b_f[o][None])
            w_out = w_out_odd[o].astype(BF16)
            q, k, kb, v, vb, fl = norm_proj(hp.reshape(bsz * lp, d), g_mix, slabs, dts, scl, tm=tm)
            r3 = lambda x: x.reshape(bsz, lp, x.shape[-1])
            logf, cum, cum_t = logf_cumsum(r3(fl), bf_row, lead=LEAD)
            yc = fox_attention(r3(q), r3(kb), r3(vb), cum, cum_t[:, :h_c, None, :], tq=tq, lead=LEAD)
            hp = out_proj(hp, g_post, [yc], [w_out], tm=tm, lead=LEAD)
            outs_p["fk"].append(r3(k)[:, LEAD:].reshape(bsz, t_p, h_c, HEAD_DIM))
            outs_p["fv"].append(r3(v)[:, LEAD:].reshape(bsz, t_p, h_c, HEAD_DIM))
            outs_p["fl"].append(logf[:, LEAD:, :h_c])
            q, k, v, fl = norm_proj(hs, g_mix, slabs, [(BF16,), (F32,), (F32,), (F32,)], scl, tm=_row_tile(db))
            yc, logf_s = fox_decode(q[:, None], _new_token_page(k), _new_token_page(v), fl[:, None], bf_row,
                                    _cache_view(cache_fox_k), _cache_view(cache_fox_v), o,
                                    jnp.swapaxes(cache_fox_logf[o], 1, 2), page_table, pp=pp)
            hs = out_proj(hs[None], g_post, [yc.reshape(1, db, d_c)], [w_out], tm=_row_tile(db), lead=0)[0]
            outs_s["fk"].append(k.reshape(db, 1, h_c, HEAD_DIM))
            outs_s["fv"].append(v.reshape(db, 1, h_c, HEAD_DIM))
            outs_s["fl"].append(logf_s[:, :, :h_c])
        hp = ffn(hp.reshape(bsz * lp, d), l, 1).reshape(bsz, lp, d)
        hs = ffn(hs, l, 1)

    y_prompt = hp[:, LEAD + N_META:]
    y_sample = hs.reshape(db, 1, d)
    order = ("sb_k", "sb_v", "fk", "fv", "fl", "rw", "sh")
    return (y_prompt, y_sample) + tuple(jnp.stack(outs_p[k]) for k in order) + tuple(
        jnp.stack(outs_s[k]) for k in order)
```
